```python
import math
import jax, jax.numpy as jnp
from jax import lax
import numpy as np

D_MODEL = 2048
BATCH = 2
SEQ = 4096
DEPTH = 1

GRID_W = 64
CTX_LEN = 256
MIX_WIDTH = D_MODEL
HG_WIDTH = MIX_WIDTH // 2
DA_WIDTH = MIX_WIDTH - HG_WIDTH
HG_EXPAND = 128
HG_HEADS = HG_WIDTH // HG_EXPAND
HG_CHUNK = 64
DA_HEAD_DIM = 64
DA_V_DIM = 2 * DA_HEAD_DIM
DA_HEADS = DA_WIDTH // DA_V_DIM
Q_BLOCK = 128
ROPE_BASE = 10000.0
PEER_HEADS = 8
PEER_NKEYS = 128
PEER_EXPERTS = PEER_NKEYS * PEER_NKEYS
PEER_QDIM = 256
PEER_TOPK = 16
PEER_BLOCK = 128
RMS_EPS = 1e-6
IN_WIDTHS = (HG_WIDTH,) * 5 + (DA_WIDTH,) * 3
IN_COLS = sum(IN_WIDTHS)
IN_SPLITS = tuple(int(s) for s in np.cumsum(IN_WIDTHS)[:-1])

kernel_name = 'hybrid_hgrn2_diffattn_peer_block'

F32 = jnp.float32


def rmsnorm(x, g):
    xf = x.astype(F32)
    y = xf * lax.rsqrt(jnp.mean(xf * xf, axis=-1, keepdims=True) + RMS_EPS)
    return (y * g.astype(F32)).astype(x.dtype)


def split_heads(t, n_heads):
    b, s, _ = t.shape
    return t.reshape(b, s, n_heads, -1).transpose(0, 2, 1, 3)


def merge_heads(t):
    b, h, s, d = t.shape
    return t.transpose(0, 2, 1, 3).reshape(b, s, h * d)


def flip_seq(t):
    return jnp.flip(t, axis=2)


def rope_1d(t, ang):
    half = t.shape[-1] // 2
    cos = jnp.cos(ang).astype(t.dtype)
    sin = jnp.sin(ang).astype(t.dtype)
    t1, t2 = t[..., :half], t[..., half:]
    return jnp.concatenate([t1 * cos - t2 * sin, t1 * sin + t2 * cos], axis=-1)


def axial_rope(t, ang_row, ang_col):
    half = t.shape[-1] // 2
    return jnp.concatenate([rope_1d(t[..., :half], ang_row), rope_1d(t[..., half:], ang_col)], axis=-1)


def chunk_recurrence(q, k, v, log_f, s0, with_output):
    b, h, t, dk = k.shape
    dv = v.shape[-1]
    n = t // HG_CHUNK

    def to_chunks(z):
        return z.reshape(b, h, n, HG_CHUNK, z.shape[-1]).transpose(2, 0, 1, 3, 4)

    mask = jnp.tril(jnp.ones((HG_CHUNK, HG_CHUNK), dtype=bool))[:, :, None]

    def step(state, inp):
        kc, vc, lfc = inp[0], inp[1], inp[2]
        cum = jnp.cumsum(lfc, axis=-2)
        last = cum[..., -1:, :]
        new_state = (jnp.exp(last[..., 0, :])[..., None] * state
                     + jnp.einsum('bhsk,bhsv->bhkv', kc * jnp.exp(last - cum), vc))
        if not with_output:
            return new_state, None
        qc = inp[3]
        diff = cum[..., :, None, :] - cum[..., None, :, :]
        decay = jnp.where(mask, jnp.exp(jnp.where(mask, diff, 0.0)), 0.0)
        scores = jnp.einsum('bhtk,bhtsk,bhsk->bhts', qc, decay, kc)
        out = (jnp.einsum('bhtk,bhkv->bhtv', qc * jnp.exp(cum), state)
               + jnp.einsum('bhts,bhsv->bhtv', scores, vc))
        return new_state, out

    inputs = (k, v, log_f) + ((q,) if with_output else ())
    final, outs = lax.scan(step, s0, tuple(to_chunks(z) for z in inputs))
    if with_output:
        outs = outs.transpose(1, 2, 0, 3, 4).reshape(b, h, t, dv)
    return outs, final


def hgrn2_mixer(lat_parts, ctx_parts, lb, norm_g, with_ctx_out):
    dtype = lat_parts[0].dtype
    lb = lb.reshape(2, HG_HEADS, 1, HG_EXPAND)

    def gates(z, lb_dir):
        f = lb_dir + (1.0 - lb_dir) * jax.nn.sigmoid(split_heads(z.astype(F32), HG_HEADS))
        return 1.0 - f, jnp.log(f)

    def readout(o, g):
        o = o * lax.rsqrt(jnp.mean(o * o, axis=-1, keepdims=True) + RMS_EPS)
        o = o * norm_g.astype(F32).reshape(HG_HEADS, 1, HG_EXPAND)
        return merge_heads(o * jax.nn.silu(split_heads(g.astype(F32), HG_HEADS))).astype(dtype)

    def scan_both(parts, s_fwd, s_bwd, with_out):
        q, zf, zb, i = parts[0], parts[1], parts[2], parts[3]
        qh = jax.nn.silu(split_heads(q.astype(F32), HG_HEADS)) if with_out else None
        ih = split_heads(i.astype(F32), HG_HEADS)
        kf, lff = gates(zf, lb[0])
        kb, lfb = gates(zb, lb[1])
        o_f, st_f = chunk_recurrence(qh, kf, ih, lff, s_fwd, with_out)
        o_b, st_b = chunk_recurrence(flip_seq(qh) if with_out else None, flip_seq(kb), flip_seq(ih),
                                     flip_seq(lfb), s_bwd, with_out)
        o = o_f + flip_seq(o_b) if with_out else None
        return o, st_f, st_b

    b = lat_parts[0].shape[0]
    s0 = jnp.zeros((b, HG_HEADS, HG_EXPAND, HG_EXPAND), F32)
    o_ctx, st_f, st_b = scan_both(ctx_parts, s0, s0, with_ctx_out)
    o_lat, _, _ = scan_both(lat_parts, st_f, st_b, True)
    out_lat = readout(o_lat, lat_parts[4])
    out_ctx = readout(o_ctx, ctx_parts[4]) if with_ctx_out else None
    return out_lat, out_ctx


def diff_attend(q1, q2, k1, k2, v, lam):
    scale = DA_HEAD_DIM ** -0.5
    p1 = jax.nn.softmax(jnp.einsum('bhqd,bhkd->bhqk', q1, k1).astype(F32) * scale, axis=-1)
    p2 = jax.nn.softmax(jnp.einsum('bhqd,bhkd->bhqk', q2, k2).astype(F32) * scale, axis=-1)
    return jnp.einsum('bhqk,bhkv->bhqv', (p1 - lam * p2).astype(v.dtype), v)


def diff_attention(lat_parts, ctx_parts, lam, lam_init, subln_g, ang_row, ang_col, with_ctx_out):
    def heads(q, k, v):
        b, t, _ = q.shape
        q = q.reshape(b, t, DA_HEADS, 2, DA_HEAD_DIM).transpose(3, 0, 2, 1, 4)
        k = k.reshape(b, t, DA_HEADS, 2, DA_HEAD_DIM).transpose(3, 0, 2, 1, 4)
        return q[0], q[1], k[0], k[1], split_heads(v, DA_HEADS)

    def subln(o):
        of = o.astype(F32)
        of = of * lax.rsqrt(jnp.mean(of * of, axis=-1, keepdims=True) + RMS_EPS)
        of = of * subln_g.astype(F32).reshape(DA_HEADS, 1, DA_V_DIM) * (1.0 - lam_init)
        return merge_heads(of).astype(o.dtype)

    lq1, lq2, lk1, lk2, lv = heads(*lat_parts)
    lq1, lq2, lk1, lk2 = [axial_rope(z, ang_row, ang_col) for z in (lq1, lq2, lk1, lk2)]
    cq1, cq2, ck1, ck2, cv = heads(*ctx_parts)
    k1 = jnp.concatenate([ck1, lk1], axis=2)
    k2 = jnp.concatenate([ck2, lk2], axis=2)
    v = jnp.concatenate([cv, lv], axis=2)
    b, h, t, d = lq1.shape
    nb = t // Q_BLOCK

    def to_blocks(z):
        return z.reshape(b, h, nb, Q_BLOCK, d).transpose(2, 0, 1, 3, 4)

    o_blocks = lax.map(lambda qs: diff_attend(qs[0], qs[1], k1, k2, v, lam), (to_blocks(lq1), to_blocks(lq2)))
    o_lat = o_blocks.transpose(1, 2, 0, 3, 4).reshape(b, h, t, DA_V_DIM)
    out_lat = subln(o_lat)
    out_ctx = subln(diff_attend(cq1, cq2, ck1, ck2, cv, lam)) if with_ctx_out else None
    return out_lat, out_ctx


def peer(h, w_query, sub_keys, u_tab, v_tab):
    b, t, d = h.shape
    blocks = h.reshape(-1, PEER_BLOCK, d)

    def block_fn(xb):
        p = xb.shape[0]
        q = (xb @ w_query).reshape(p, PEER_HEADS, 2, PEER_QDIM // 2)
        s = jnp.einsum('phjd,hjnd->phjn', q, sub_keys).astype(F32)
        sv, si = lax.top_k(s, PEER_TOPK)
        cand = (sv[:, :, 0, :, None] + sv[:, :, 1, None, :]).reshape(p, PEER_HEADS, PEER_TOPK * PEER_TOPK)
        best, ci = lax.top_k(cand, PEER_TOPK)
        i1 = jnp.take_along_axis(si[:, :, 0, :], ci // PEER_TOPK, axis=-1)
        i2 = jnp.take_along_axis(si[:, :, 1, :], ci % PEER_TOPK, axis=-1)
        idx = i1 * PEER_NKEYS + i2
        g = jax.nn.softmax(best, axis=-1)
        act = jax.nn.gelu(jnp.einsum('pd,phkd->phk', xb, u_tab[idx]).astype(F32), approximate=False)
        return jnp.einsum('phk,phkd->pd', (g * act).astype(xb.dtype), v_tab[idx])

    return lax.map(block_fn, blocks).reshape(b, t, d)


def setup_inputs(seed: int = 0) -> dict:
    key = jax.random.key(seed)
    ks = jax.random.split(key, 22)
    L = DEPTH

    def nrm(k, shape, s):
        return jax.random.normal(k, shape, jnp.float32) * s

    return {
        'x': nrm(ks[0], (BATCH, SEQ, D_MODEL), 1.0),
        'c': nrm(ks[1], (BATCH, D_MODEL), 1.0),
        'ctx': nrm(ks[2], (BATCH, CTX_LEN, D_MODEL), 1.0),
        'c_ctx': nrm(ks[3], (D_MODEL,), 1.0),
        'w_ada': nrm(ks[4], (L, D_MODEL, 6 * D_MODEL), 0.5 * D_MODEL ** -0.5),
        'b_ada': nrm(ks[5], (L, 6 * D_MODEL), 0.01),
        'norm1_g': 1.0 + nrm(ks[6], (L, D_MODEL), 0.1),
        'w_in': nrm(ks[7], (L, D_MODEL, IN_COLS), D_MODEL ** -0.5),
        'hg_gamma': nrm(ks[8], (L + 1, 2, HG_WIDTH), 0.5),
        'hg_norm_g': 1.0 + nrm(ks[9], (L, HG_WIDTH), 0.1),
        'da_lambda_q1': nrm(ks[10], (L, DA_HEAD_DIM), 0.1),
        'da_lambda_k1': nrm(ks[11], (L, DA_HEAD_DIM), 0.1),
        'da_lambda_q2': nrm(ks[12], (L, DA_HEAD_DIM), 0.1),
        'da_lambda_k2': nrm(ks[13], (L, DA_HEAD_DIM), 0.1),
        'da_subln_g': 1.0 + nrm(ks[14], (L, DA_WIDTH), 0.1),
        'w_out': nrm(ks[15], (L, MIX_WIDTH, D_MODEL), MIX_WIDTH ** -0.5),
        'norm2_g': 1.0 + nrm(ks[16], (L, D_MODEL), 0.1),
        'peer_w_query': nrm(ks[17], (L, D_MODEL, PEER_HEADS * PEER_QDIM), D_MODEL ** -0.5),
        'peer_sub_keys': nrm(ks[18], (L, PEER_HEADS, 2, PEER_NKEYS, PEER_QDIM // 2), (PEER_QDIM // 2) ** -0.5),
        'peer_u': nrm(ks[19], (L, PEER_EXPERTS, D_MODEL), D_MODEL ** -0.5),
        'peer_v': nrm(ks[20], (L, PEER_EXPERTS, D_MODEL), 0.5),
        'final_norm_g': 1.0 + nrm(ks[21], (D_MODEL,), 0.1),
    }


def reference(x, c, ctx, c_ctx, w_ada, b_ada, norm1_g, w_in, hg_gamma, hg_norm_g,
              da_lambda_q1, da_lambda_k1, da_lambda_q2, da_lambda_k2, da_subln_g, w_out,
              norm2_g, peer_w_query, peer_sub_keys, peer_u, peer_v, final_norm_g):
    n_lat = x.shape[1]
    ROWS = n_lat // GRID_W
    row = jnp.repeat(jnp.arange(ROWS, dtype=F32), GRID_W)
    col = jnp.tile(jnp.arange(GRID_W, dtype=F32), ROWS)
    n_freq = DA_HEAD_DIM // 4
    inv_freq = ROPE_BASE ** (-jnp.arange(n_freq, dtype=F32) / n_freq)
    ang_row = row[:, None] * inv_freq[None, :]
    ang_col = col[:, None] * inv_freq[None, :]
    lb_all = jnp.cumsum(jax.nn.softmax(hg_gamma.astype(F32), axis=0), axis=0)
    c_act = jax.nn.silu(c)[:, None, :]
    cc_act = jax.nn.silu(c_ctx)[None, None, :]
    xc = ctx
    for l in range(DEPTH):
        with_ctx_out = l < DEPTH - 1
        sh1, sc1, g1, sh2, sc2, g2 = jnp.split(c_act @ w_ada[l] + b_ada[l], 6, axis=-1)
        csh1, csc1, cg1, csh2, csc2, cg2 = jnp.split(cc_act @ w_ada[l] + b_ada[l], 6, axis=-1)
        h = rmsnorm(x, norm1_g[l]) * (1.0 + sc1) + sh1
        hc = rmsnorm(xc, norm1_g[l]) * (1.0 + csc1) + csh1
        parts = jnp.split(h @ w_in[l], IN_SPLITS, axis=-1)
        cparts = jnp.split(hc @ w_in[l], IN_SPLITS, axis=-1)
        lam_init = 0.8 - 0.6 * math.exp(-0.3 * l)
        lam = (jnp.exp(jnp.sum(da_lambda_q1[l].astype(F32) * da_lambda_k1[l].astype(F32)))
               - jnp.exp(jnp.sum(da_lambda_q2[l].astype(F32) * da_lambda_k2[l].astype(F32))) + lam_init)
        hg_lat, hg_ctx = hgrn2_mixer(parts[:5], cparts[:5], lb_all[l], hg_norm_g[l], with_ctx_out)
        da_lat, da_ctx = diff_attention(parts[5:], cparts[5:], lam, lam_init, da_subln_g[l],
                                        ang_row, ang_col, with_ctx_out)
        x = x + g1 * (jnp.concatenate([hg_lat, da_lat], axis=-1) @ w_out[l])
        x = x + g2 * peer(rmsnorm(x, norm2_g[l]) * (1.0 + sc2) + sh2,
                          peer_w_query[l], peer_sub_keys[l], peer_u[l], peer_v[l])
        if with_ctx_out:
            xc = xc + cg1 * (jnp.concatenate([hg_ctx, da_ctx], axis=-1) @ w_out[l])
            xc = xc + cg2 * peer(rmsnorm(xc, norm2_g[l]) * (1.0 + csc2) + csh2,
                                 peer_w_query[l], peer_sub_keys[l], peer_u[l], peer_v[l])
    return rmsnorm(x, final_norm_g)
```

```python
import functools
import math

import jax
import jax.numpy as jnp
from jax import lax
from jax.experimental import pallas as pl
from jax.experimental.pallas import tpu as pltpu

F32 = jnp.float32
BF16 = jnp.bfloat16

GRID_W = 64
HG_EXPAND = 128
HG_CHUNK = 64
DA_HEAD_DIM = 64
DA_V_DIM = 128
ROPE_BASE = 10000.0
PEER_HEADS = 8
PEER_NKEYS = 128
PEER_TOPK = 16
RMS_EPS = 1e-6
LANES = 128
EXP_CLAMP = 80.0
VMEM_LIMIT = 56 * 1024 * 1024
NEG_INF = float("-inf")


def _nt(a, b):
    return lax.dot_general(a, b, (((1,), (1,)), ((), ())), preferred_element_type=F32)


def _tn(a, b):
    return lax.dot_general(a, b, (((0,), (0,)), ((), ())), preferred_element_type=F32)


def _nn(a, b):
    return jnp.dot(a, b, preferred_element_type=F32)


def _cparams(sem):
    return pltpu.CompilerParams(dimension_semantics=sem, vmem_limit_bytes=VMEM_LIMIT)


def _adaln_kernel(c_ref, w_ref, b_ref, o_ref):
    c = c_ref[...]
    a = (c * jax.nn.sigmoid(c)).astype(BF16)
    o_ref[...] = _nn(a, w_ref[...].astype(BF16)) + b_ref[...]


def _adaln(cond, w, b):
    d, n = w.shape
    tn = 1024
    return pl.pallas_call(
        _adaln_kernel,
        grid=(n // tn,),
        in_specs=[pl.BlockSpec((8, d), lambda j: (0, 0)),
                  pl.BlockSpec((d, tn), lambda j: (0, j)),
                  pl.BlockSpec((1, tn), lambda j: (0, j))],
        out_specs=pl.BlockSpec((8, tn), lambda j: (0, j)),
        out_shape=jax.ShapeDtypeStruct((8, n), F32),
        compiler_params=_cparams(("arbitrary",)),
        name="adaln",
    )(cond, w, b.reshape(1, n))


def _inproj_kernel(x_ref, sc_ref, sh_ref, g_ref, w_ref, o_ref, h_scr):
    @pl.when(pl.program_id(1) == 0)
    def _():
        x = x_ref[...]
        y = x * lax.rsqrt(jnp.mean(x * x, axis=-1, keepdims=True) + RMS_EPS) * g_ref[...]
        h_scr[...] = (y * (1.0 + sc_ref[0]) + sh_ref[0]).astype(BF16)

    o_ref[...] = _nn(h_scr[...], w_ref[...]).astype(BF16)


def _inproj(x2, sc, sh, g, w, rows_per_mod, tm, tn):
    m, d = x2.shape
    n = w.shape[1]
    bpm = rows_per_mod // tm
    return pl.pallas_call(
        _inproj_kernel,
        grid=(m // tm, n // tn),
        in_specs=[pl.BlockSpec((tm, d), lambda i, j: (i, 0)),
                  pl.BlockSpec((1, 1, d), lambda i, j: (i // bpm, 0, 0)),
                  pl.BlockSpec((1, 1, d), lambda i, j: (i // bpm, 0, 0)),
                  pl.BlockSpec((1, d), lambda i, j: (0, 0)),
                  pl.BlockSpec((d, tn), lambda i, j: (0, j))],
        out_specs=pl.BlockSpec((tm, tn), lambda i, j: (i, j)),
        out_shape=jax.ShapeDtypeStruct((m, n), BF16),
        scratch_shapes=[pltpu.VMEM((tm, d), BF16)],
        compiler_params=_cparams(("parallel", "arbitrary")),
        name="inproj",
    )(x2, sc, sh, g, w)


def _hgrn_chunk(st, z, v_bf, q, lb, tri, mask, rev):
    c = HG_CHUNK
    f = lb + (1.0 - lb) * jax.nn.sigmoid(z)
    k = 1.0 - f
    lf = jnp.log(f)
    hi = lf.astype(BF16)
    lo = (lf - hi.astype(F32)).astype(BF16)
    cum = _nn(tri, hi) + _nn(tri, lo)
    tot = cum[0:1, :] if rev else cum[c - 1:c, :]
    k_st = k * jnp.exp(tot - cum)
    st_new = st * jnp.exp(tot) + _tn(v_bf, k_st.astype(BF16))
    if q is None:
        return st_new, None
    mid = cum[c // 2:c // 2 + 1, :] if rev else cum[c // 2 - 1:c // 2, :]
    qt = q * jnp.exp(jnp.minimum(cum - mid, EXP_CLAMP))
    kt = k * jnp.exp(jnp.minimum(mid - cum, EXP_CLAMP))
    a = jnp.where(mask, _nt(qt.astype(BF16), kt.astype(BF16)), 0.0)
    o = _nn(a.astype(BF16), v_bf) + _nt((q * jnp.exp(cum)).astype(BF16), st.astype(BF16))
    return st_new, o


def _hgrn_kernel(lb_ref, ng_ref, czf_ref, czb_ref, ci_ref, q_ref, zf_ref, zb_ref, i_ref, g_ref,
                 o_ref, of_scr, ob_scr):
    c = HG_CHUNK
    n_ctx = czf_ref.shape[0] // c
    n_lat = q_ref.shape[0] // c
    row = lax.broadcasted_iota(jnp.int32, (c, c), 0)
    col = lax.broadcasted_iota(jnp.int32, (c, c), 1)
    lower = row >= col
    upper = col >= row
    tri_f = jnp.where(lower, 1.0, 0.0).astype(BF16)
    tri_b = jnp.where(upper, 1.0, 0.0).astype(BF16)
    lb_f = lb_ref[0:1, :]
    lb_b = lb_ref[1:2, :]

    def ctx_body(j, carry):
        sf, sb = carry
        rf = pl.ds(pl.multiple_of(j * c, c), c)
        rb = pl.ds(pl.multiple_of((n_ctx - 1 - j) * c, c), c)
        sf, _ = _hgrn_chunk(sf, czf_ref[rf, :].astype(F32), ci_ref[rf, :], None, lb_f, tri_f, lower, False)
        sb, _ = _hgrn_chunk(sb, czb_ref[rb, :].astype(F32), ci_ref[rb, :], None, lb_b, tri_b, upper, True)
        return sf, sb

    s0 = jnp.zeros((HG_EXPAND, HG_EXPAND), F32)
    sf, sb = lax.fori_loop(0, n_ctx, ctx_body, (s0, s0))

    def lat_body(j, carry):
        sf, sb = carry
        rf = pl.ds(pl.multiple_of(j * c, c), c)
        rb = pl.ds(pl.multiple_of((n_lat - 1 - j) * c, c), c)
        qf = q_ref[rf, :].astype(F32)
        qf = qf * jax.nn.sigmoid(qf)
        qb = q_ref[rb, :].astype(F32)
        qb = qb * jax.nn.sigmoid(qb)
        sf, o_f = _hgrn_chunk(sf, zf_ref[rf, :].astype(F32), i_ref[rf, :], qf, lb_f, tri_f, lower, False)
        sb, o_b = _hgrn_chunk(sb, zb_ref[rb, :].astype(F32), i_ref[rb, :], qb, lb_b, tri_b, upper, True)
        of_scr[rf, :] = o_f
        ob_scr[rb, :] = o_b
        return sf, sb

    lax.fori_loop(0, n_lat, lat_body, (sf, sb))

    rows = 256
    ng = ng_ref[...]

    def out_body(j, _):
        r = pl.ds(pl.multiple_of(j * rows, rows), rows)
        o = of_scr[r, :] + ob_scr[r, :]
        o = o * lax.rsqrt(jnp.mean(o * o, axis=-1, keepdims=True) + RMS_EPS) * ng
        g = g_ref[r, :].astype(F32)
        o_ref[r, :] = (o * (g * jax.nn.sigmoid(g))).astype(BF16)
        return 0

    lax.fori_loop(0, q_ref.shape[0] // rows, out_body, 0)


def _hgrn(cparts, parts, lb, norm_g, n_heads):
    b, t, _ = parts.shape
    tc = cparts.shape[1]
    w = HG_EXPAND

    def col(part):
        return lambda bi, hi: (bi, 0, part * n_heads + hi)

    cspec = lambda part: pl.BlockSpec((None, tc, w), col(part))
    lspec = lambda part: pl.BlockSpec((None, t, w), col(part))
    return pl.pallas_call(
        _hgrn_kernel,
        grid=(b, n_heads),
        in_specs=[pl.BlockSpec((2, w), lambda bi, hi: (0, hi)),
                  pl.BlockSpec((1, w), lambda bi, hi: (0, hi)),
                  cspec(1), cspec(2), cspec(3),
                  lspec(0), lspec(1), lspec(2), lspec(3), lspec(4)],
        out_specs=pl.BlockSpec((None, t, w), lambda bi, hi: (bi, 0, hi)),
        out_shape=jax.ShapeDtypeStruct((b, t, n_heads * w), BF16),
        scratch_shapes=[pltpu.VMEM((t, w), F32), pltpu.VMEM((t, w), F32)],
        compiler_params=_cparams(("parallel", "parallel")),
        name="hgrn2",
    )(lb, norm_g.reshape(1, -1), cparts, cparts, cparts, parts, parts, parts, parts, parts)


def _rope(x, cos, sin):
    lane = lax.broadcasted_iota(jnp.int32, x.shape, 1)
    swapped = jnp.where((lane % 32) < 16,
                        pltpu.roll(x, LANES - 16, axis=1),
                        pltpu.roll(x, 16, axis=1))
    return x * cos + swapped * sin


def _da_kernel(lam_ref, q_ref, kc_ref, kl_ref, vc_ref, vl_ref, cq_ref, sq_ref, ck_ref, sk_ref, g_ref,
               o_ref, k_scr, v_scr):
    tc = kc_ref.shape[0]
    t = kl_ref.shape[0]

    @pl.when(pl.program_id(2) == 0)
    def _():
        k_scr[0:tc, :] = kc_ref[...]
        v_scr[0:tc, :] = vc_ref[...]
        rows = min(512, t)

        def body(j, _):
            r = pl.ds(pl.multiple_of(j * rows, rows), rows)
            ro = pl.ds(pl.multiple_of(tc + j * rows, 128), rows)
            k_scr[ro, :] = _rope(kl_ref[r, :].astype(F32), ck_ref[r, :], sk_ref[r, :]).astype(BF16)
            v_scr[ro, :] = vl_ref[r, :]
            return 0

        lax.fori_loop(0, t // rows, body, 0)

    lam = lam_ref[0]
    out_scale = lam_ref[1]
    qscale = (DA_HEAD_DIM ** -0.5) * math.log2(math.e)
    qr = _rope(q_ref[...].astype(F32), cq_ref[...], sq_ref[...]) * qscale
    lane = lax.broadcasted_iota(jnp.int32, qr.shape, 1)
    q1 = jnp.where(lane < DA_HEAD_DIM, qr, 0.0).astype(BF16)
    q2 = jnp.where(lane >= DA_HEAD_DIM, qr, 0.0).astype(BF16)
    k = k_scr[...]
    s1 = _nt(q1, k)
    s2 = _nt(q2, k)
    p1 = jnp.exp2(s1 - jnp.max(s1, axis=-1, keepdims=True))
    p2 = jnp.exp2(s2 - jnp.max(s2, axis=-1, keepdims=True))
    r1 = 1.0 / jnp.sum(p1, axis=-1, keepdims=True)
    r2 = lam / jnp.sum(p2, axis=-1, keepdims=True)
    p = (p1 * r1 - p2 * r2).astype(BF16)
    o = _nn(p, v_scr[...])
    o = o * lax.rsqrt(jnp.mean(o * o, axis=-1, keepdims=True) + RMS_EPS)
    o_ref[...] = (o * g_ref[...] * out_scale).astype(BF16)


def _diff_attn(cparts, parts, lam2, subln_g, cos_t, sin_t, n_heads, col0, tq):
    b, t, _ = parts.shape
    tc = cparts.shape[1]
    w = LANES

    def col(part):
        return lambda bi, hi, qi: (bi, 0, col0 + part * n_heads + hi)

    return pl.pallas_call(
        _da_kernel,
        grid=(b, n_heads, t // tq),
        in_specs=[pl.BlockSpec(memory_space=pltpu.SMEM),
                  pl.BlockSpec((None, tq, w), lambda bi, hi, qi: (bi, qi, col0 + hi)),
                  pl.BlockSpec((None, tc, w), col(1)),
                  pl.BlockSpec((None, t, w), col(1)),
                  pl.BlockSpec((None, tc, w), col(2)),
                  pl.BlockSpec((None, t, w), col(2)),
                  pl.BlockSpec((tq, w), lambda bi, hi, qi: (qi, 0)),
                  pl.BlockSpec((tq, w), lambda bi, hi, qi: (qi, 0)),
                  pl.BlockSpec((t, w), lambda bi, hi, qi: (0, 0)),
                  pl.BlockSpec((t, w), lambda bi, hi, qi: (0, 0)),
                  pl.BlockSpec((1, w), lambda bi, hi, qi: (0, hi))],
        out_specs=pl.BlockSpec((None, tq, w), lambda bi, hi, qi: (bi, qi, hi)),
        out_shape=jax.ShapeDtypeStruct((b, t, n_heads * w), BF16),
        scratch_shapes=[pltpu.VMEM((tc + t, w), BF16), pltpu.VMEM((tc + t, w), BF16)],
        compiler_params=_cparams(("parallel", "parallel", "arbitrary")),
        name="diff_attn",
    )(lam2, parts, cparts, parts, cparts, parts, cos_t, sin_t, cos_t, sin_t, subln_g.reshape(1, -1))


def _outproj_kernel(hg_ref, da_ref, x_ref, wa_ref, wb_ref, g1_ref, sc_ref, sh_ref, ng_ref, x1_ref, h2_ref):
    mix = _nn(hg_ref[...], wa_ref[...]) + _nn(da_ref[...], wb_ref[...])
    x1 = x_ref[...] + g1_ref[0] * mix
    x1_ref[...] = x1
    y = x1 * lax.rsqrt(jnp.mean(x1 * x1, axis=-1, keepdims=True) + RMS_EPS) * ng_ref[...]
    h2_ref[...] = (y * (1.0 + sc_ref[0]) + sh_ref[0]).astype(BF16)


def _outproj(hg, da, x2, w_out, g1, sc2, sh2, norm_g, rows_per_mod, tm):
    m, d = x2.shape
    half = hg.shape[1]
    bpm = rows_per_mod // tm
    mod = pl.BlockSpec((1, 1, d), lambda i: (i // bpm, 0, 0))
    return pl.pallas_call(
        _outproj_kernel,
        grid=(m // tm,),
        in_specs=[pl.BlockSpec((tm, half), lambda i: (i, 0)),
                  pl.BlockSpec((tm, half), lambda i: (i, 0)),
                  pl.BlockSpec((tm, d), lambda i: (i, 0)),
                  pl.BlockSpec((half, d), lambda i: (0, 0)),
                  pl.BlockSpec((half, d), lambda i: (1, 0)),
                  mod, mod, mod,
                  pl.BlockSpec((1, d), lambda i: (0, 0))],
        out_specs=[pl.BlockSpec((tm, d), lambda i: (i, 0)),
                   pl.BlockSpec((tm, d), lambda i: (i, 0))],
        out_shape=[jax.ShapeDtypeStruct((m, d), F32), jax.ShapeDtypeStruct((m, d), BF16)],
        compiler_params=_cparams(("parallel",)),
        name="outproj",
    )(hg, da, x2, w_out, w_out, g1, sc2, sh2, norm_g)


def _top_values(s, n):
    out = []
    for _ in range(n):
        m = jnp.max(s, axis=0, keepdims=True)
        out.append(m)
        s = jnp.where(s == m, NEG_INF, s)
    return out


def _route_kernel(h_ref, wq_ref, sk_ref, st_ref, stat_ref):
    kk = PEER_TOPK
    q = _nn(h_ref[...], wq_ref[...]).astype(BF16)
    for h in range(PEER_HEADS):
        s1 = _nt(sk_ref[2 * h], q[:, (2 * h) * LANES:(2 * h + 1) * LANES])
        s2 = _nt(sk_ref[2 * h + 1], q[:, (2 * h + 1) * LANES:(2 * h + 2) * LANES])
        st_ref[2 * h] = s1
        st_ref[2 * h + 1] = s2
        v1 = _top_values(s1, kk)
        v2 = _top_values(s2, kk)
        sv2 = jnp.concatenate(v2, axis=0)
        cands = [v1[0] + sv2]
        cands += [v1[r] + sv2[0:8, :] for r in range(1, 8)]
        cands.append(jnp.concatenate(v1[8:16], axis=0) + v2[0])
        top = v1[0] + v2[0]
        z = jnp.zeros_like(top)
        m = top
        for _ in range(kk):
            m8 = cands[1]
            for cnd in cands[2:]:
                m8 = jnp.maximum(m8, cnd)
            m = jnp.maximum(jnp.max(cands[0], axis=0, keepdims=True), jnp.max(m8, axis=0, keepdims=True))
            z = z + jnp.exp(m - top)
            cands = [jnp.where(cnd == m, NEG_INF, cnd) for cnd in cands]
        stat_ref[h, 0:1, :] = v1[0]
        stat_ref[h, 1:2, :] = v2[0]
        stat_ref[h, 2:3, :] = 1.0 / z
        stat_ref[h, 3:4, :] = m
        stat_ref[h, 4:8, :] = jnp.zeros((4, m.shape[1]), F32)


def _route(h2, wq, sk, tp):
    p, d = h2.shape
    nq = wq.shape[1]
    return pl.pallas_call(
        _route_kernel,
        grid=(p // tp,),
        in_specs=[pl.BlockSpec((tp, d), lambda i: (i, 0)),
                  pl.BlockSpec((d, nq), lambda i: (0, 0)),
                  pl.BlockSpec((2 * PEER_HEADS, PEER_NKEYS, LANES), lambda i: (0, 0, 0))],
        out_specs=[pl.BlockSpec((2 * PEER_HEADS, PEER_NKEYS, tp), lambda i: (0, 0, i)),
                   pl.BlockSpec((PEER_HEADS, 8, tp), lambda i: (0, 0, i))],
        out_shape=[jax.ShapeDtypeStruct((2 * PEER_HEADS, PEER_NKEYS, p), F32),
                   jax.ShapeDtypeStruct((PEER_HEADS, 8, p), F32)],
        compiler_params=_cparams(("parallel",)),
        name="peer_route",
    )(h2, wq, sk)


def _peer_kernel(x1_ref, g2_ref, fg_ref, h_ref, st_ref, stat_ref, u_ref, v_ref, o_ref,
                 acc, a_scr, b_scr, ht_scr):
    e = pl.program_id(1)
    n_slab = u_ref.shape[0] // PEER_NKEYS

    @pl.when(e == 0)
    def _():
        acc[...] = jnp.zeros_like(acc)
        for h in range(PEER_HEADS):
            a_scr[h] = jnp.exp(st_ref[2 * h] - stat_ref[h, 0:1, :])
            b_scr[h] = jnp.exp(st_ref[2 * h + 1] - stat_ref[h, 1:2, :]) * stat_ref[h, 2:3, :]

    act = _nt(u_ref[...], h_ref[...])
    for sl in range(n_slab):
        i1 = e * n_slab + sl
        gate = None
        for h in range(PEER_HEADS):
            cand = st_ref[2 * h, pl.ds(i1, 1), :] + st_ref[2 * h + 1]
            term = a_scr[h, pl.ds(i1, 1), :] * jnp.where(cand >= stat_ref[h, 3:4, :], b_scr[h], 0.0)
            gate = term if gate is None else gate + term
        rows = slice(sl * PEER_NKEYS, (sl + 1) * PEER_NKEYS)
        a = act[rows, :]
        gelu = 0.5 * a * (1.0 + lax.erf(a * math.sqrt(0.5)))
        ht_scr[rows, :] = (gate * gelu).astype(BF16)
    acc[...] += _tn(ht_scr[...], v_ref[...])

    @pl.when(e == pl.num_programs(1) - 1)
    def _():
        x = x1_ref[...] + g2_ref[0] * acc[...]
        o_ref[...] = x * lax.rsqrt(jnp.mean(x * x, axis=-1, keepdims=True) + RMS_EPS) * fg_ref[...]


def _peer(x1, g2, fg, h2, st, stat, u_bf, v_bf, rows_per_mod, tp, te):
    p, d = x1.shape
    n_exp = u_bf.shape[0]
    bpm = rows_per_mod // tp
    return pl.pallas_call(
        _peer_kernel,
        grid=(p // tp, n_exp // te),
        in_specs=[pl.BlockSpec((tp, d), lambda i, e: (i, 0)),
                  pl.BlockSpec((1, 1, d), lambda i, e: (i // bpm, 0, 0)),
                  pl.BlockSpec((1, d), lambda i, e: (0, 0)),
                  pl.BlockSpec((tp, d), lambda i, e: (i, 0)),
                  pl.BlockSpec((2 * PEER_HEADS, PEER_NKEYS, tp), lambda i, e: (0, 0, i)),
                  pl.BlockSpec((PEER_HEADS, 8, tp), lambda i, e: (0, 0, i)),
                  pl.BlockSpec((te, d), lambda i, e: (e, 0)),
                  pl.BlockSpec((te, d), lambda i, e: (e, 0))],
        out_specs=pl.BlockSpec((tp, d), lambda i, e: (i, 0)),
        out_shape=jax.ShapeDtypeStruct((p, d), F32),
        scratch_shapes=[pltpu.VMEM((tp, d), F32),
                        pltpu.VMEM((PEER_HEADS, PEER_NKEYS, tp), F32),
                        pltpu.VMEM((PEER_HEADS, PEER_NKEYS, tp), F32),
                        pltpu.VMEM((te, tp), BF16)],
        compiler_params=_cparams(("parallel", "arbitrary")),
        name="peer_experts",
    )(x1, g2, fg, h2, st, stat, u_bf, v_bf)


def _rope_tables(t):
    rows = t // GRID_W
    row = jnp.repeat(jnp.arange(rows, dtype=F32), GRID_W)
    colp = jnp.tile(jnp.arange(GRID_W, dtype=F32), rows)
    n_freq = DA_HEAD_DIM // 4
    inv_freq = ROPE_BASE ** (-jnp.arange(n_freq, dtype=F32) / n_freq)
    ar = row[:, None] * inv_freq[None, :]
    ac = colp[:, None] * inv_freq[None, :]
    cos64 = jnp.concatenate([jnp.cos(ar), jnp.cos(ar), jnp.cos(ac), jnp.cos(ac)], axis=-1)
    sin64 = jnp.concatenate([-jnp.sin(ar), jnp.sin(ar), -jnp.sin(ac), jnp.sin(ac)], axis=-1)
    return jnp.tile(cos64, (1, 2)), jnp.tile(sin64, (1, 2))


def _layer(x, c, ctx, c_ctx, w_ada, b_ada, norm1_g, w_in, lb, hg_norm_g, lam, lam_init, da_subln_g,
           w_out, norm2_g, peer_w_query, peer_sub_keys, peer_u, peer_v, final_norm_g):
    b, t, d = x.shape
    tc = ctx.shape[1]
    hg_width = hg_norm_g.shape[0]
    hg_heads = hg_width // HG_EXPAND
    da_heads = da_subln_g.shape[0] // DA_V_DIM

    cond = jnp.zeros((8, d), F32).at[0:b].set(c).at[b].set(c_ctx)
    mod = _adaln(cond, w_ada, b_ada)
    sh1, sc1, g1, sh2, sc2, g2 = [m[0:b].reshape(b, 1, d) for m in jnp.split(mod, 6, axis=-1)]
    csh1, csc1 = [m[b:b + 1].reshape(1, 1, d) for m in jnp.split(mod, 6, axis=-1)[0:2]]

    w_in_bf = w_in.astype(BF16)
    n_cols = w_in.shape[1]
    g1n = norm1_g.reshape(1, d)
    tm_lat = min(1024, t)
    parts = _inproj(x.reshape(b * t, d), sc1, sh1, g1n, w_in_bf, t, tm_lat, 1024).reshape(b, t, n_cols)
    tm_ctx = min(512, b * tc)
    cparts = _inproj(ctx.reshape(b * tc, d), csc1, csh1, g1n, w_in_bf, b * tc, tm_ctx, 1024).reshape(b, tc, n_cols)

    hg = _hgrn(cparts, parts, lb, hg_norm_g, hg_heads)
    cos_t, sin_t = _rope_tables(t)
    lam2 = jnp.stack([lam, jnp.asarray(1.0 - lam_init, F32)]).astype(F32)
    da = _diff_attn(cparts, parts, lam2, da_subln_g, cos_t, sin_t, da_heads, 5 * hg_heads, min(256, t))

    x2 = x.reshape(b * t, d)
    x1, h2 = _outproj(hg.reshape(b * t, -1), da.reshape(b * t, -1), x2, w_out.astype(BF16),
                      g1, sc2, sh2, norm2_g.reshape(1, d), t, min(512, t))

    sk = peer_sub_keys.reshape(2 * PEER_HEADS, PEER_NKEYS, -1).astype(BF16)
    st, stat = _route(h2, peer_w_query.astype(BF16), sk, min(256, t))
    out = _peer(x1, g2, final_norm_g.reshape(1, d), h2, st, stat,
                peer_u.astype(BF16), peer_v.astype(BF16), t, min(512, t), 512)
    return out.reshape(b, t, d)


def kernel(x, c, ctx, c_ctx, w_ada, b_ada, norm1_g, w_in, hg_gamma, hg_norm_g, da_lambda_q1, da_lambda_k1,
           da_lambda_q2, da_lambda_k2, da_subln_g, w_out, norm2_g, peer_w_query, peer_sub_keys, peer_u,
           peer_v, final_norm_g):
    depth = w_ada.shape[0]
    assert depth == 1, "single-layer block: the context stream is never updated"
    lb_all = jnp.cumsum(jax.nn.softmax(hg_gamma.astype(F32), axis=0), axis=0)
    l = 0
    lam_init = 0.8 - 0.6 * math.exp(-0.3 * l)
    lam = (jnp.exp(jnp.sum(da_lambda_q1[l].astype(F32) * da_lambda_k1[l].astype(F32)))
           - jnp.exp(jnp.sum(da_lambda_q2[l].astype(F32) * da_lambda_k2[l].astype(F32))) + lam_init)
    return _layer(x, c, ctx, c_ctx, w_ada[l], b_ada[l], norm1_g[l], w_in[l], lb_all[l], hg_norm_g[l],
                  lam, lam_init, da_subln_g[l], w_out[l], norm2_g[l], peer_w_query[l], peer_sub_keys[l],
                  peer_u[l], peer_v[l], final_norm_g)
```

```python
import functools
import math

import jax
import jax.numpy as jnp
from jax import lax
from jax.experimental import pallas as pl
from jax.experimental.pallas import tpu as pltpu

F32 = jnp.float32
BF16 = jnp.bfloat16

GRID_W = 64
HG_EXPAND = 128
HG_CHUNK = 64
DA_HEAD_DIM = 64
DA_V_DIM = 128
ROPE_BASE = 10000.0
PEER_HEADS = 8
PEER_NKEYS = 128
PEER_TOPK = 16
RMS_EPS = 1e-6
LANES = 128
EXP_CLAMP = 80.0
VMEM_LIMIT = 56 * 1024 * 1024
NEG_INF = float("-inf")


def _nt(a, b):
    return lax.dot_general(a, b, (((1,), (1,)), ((), ())), preferred_element_type=F32)


def _tn(a, b):
    return lax.dot_general(a, b, (((0,), (0,)), ((), ())), preferred_element_type=F32)


def _nn(a, b):
    return jnp.dot(a, b, preferred_element_type=F32)


def _cparams(sem):
    return pltpu.CompilerParams(dimension_semantics=sem, vmem_limit_bytes=VMEM_LIMIT)


def _adaln_kernel(c_ref, w_ref, b_ref, o_ref):
    c = c_ref[...]
    a = (c * jax.nn.sigmoid(c)).astype(BF16)
    o_ref[...] = _nn(a, w_ref[...].astype(BF16)) + b_ref[...]


def _adaln(cond, w, b):
    d, n = w.shape
    tn = 1024
    return pl.pallas_call(
        _adaln_kernel,
        grid=(n // tn,),
        in_specs=[pl.BlockSpec((8, d), lambda j: (0, 0)),
                  pl.BlockSpec((d, tn), lambda j: (0, j)),
                  pl.BlockSpec((1, tn), lambda j: (0, j))],
        out_specs=pl.BlockSpec((8, tn), lambda j: (0, j)),
        out_shape=jax.ShapeDtypeStruct((8, n), F32),
        compiler_params=_cparams(("arbitrary",)),
        name="adaln",
    )(cond, w, b.reshape(1, n))


def _inproj_kernel(x_ref, sc_ref, sh_ref, g_ref, w_ref, o_ref, h_scr):
    @pl.when(pl.program_id(1) == 0)
    def _():
        x = x_ref[...]
        y = x * lax.rsqrt(jnp.mean(x * x, axis=-1, keepdims=True) + RMS_EPS) * g_ref[...]
        h_scr[...] = (y * (1.0 + sc_ref[0]) + sh_ref[0]).astype(BF16)

    o_ref[...] = _nn(h_scr[...], w_ref[...]).astype(BF16)


def _inproj(x2, sc, sh, g, w, rows_per_mod, tm, tn):
    m, d = x2.shape
    n = w.shape[1]
    bpm = rows_per_mod // tm
    return pl.pallas_call(
        _inproj_kernel,
        grid=(m // tm, n // tn),
        in_specs=[pl.BlockSpec((tm, d), lambda i, j: (i, 0)),
                  pl.BlockSpec((1, 1, d), lambda i, j: (i // bpm, 0, 0)),
                  pl.BlockSpec((1, 1, d), lambda i, j: (i // bpm, 0, 0)),
                  pl.BlockSpec((1, d), lambda i, j: (0, 0)),
                  pl.BlockSpec((d, tn), lambda i, j: (0, j))],
        out_specs=pl.BlockSpec((tm, tn), lambda i, j: (i, j)),
        out_shape=jax.ShapeDtypeStruct((m, n), BF16),
        scratch_shapes=[pltpu.VMEM((tm, d), BF16)],
        compiler_params=_cparams(("parallel", "arbitrary")),
        name="inproj",
    )(x2, sc, sh, g, w)


def _hgrn_chunk(st, z, v_bf, q, lb, tri, mask, rev):
    c = HG_CHUNK
    f = lb + (1.0 - lb) * jax.nn.sigmoid(z)
    k = 1.0 - f
    lf = jnp.log(f)
    hi = lf.astype(BF16)
    lo = (lf - hi.astype(F32)).astype(BF16)
    cum = _nn(tri, hi) + _nn(tri, lo)
    tot = cum[0:1, :] if rev else cum[c - 1:c, :]
    k_st = k * jnp.exp(tot - cum)
    st_new = st * jnp.exp(tot) + _tn(v_bf, k_st.astype(BF16))
    if q is None:
        return st_new, None
    mid = cum[c // 2:c // 2 + 1, :] if rev else cum[c // 2 - 1:c // 2, :]
    qt = q * jnp.exp(jnp.minimum(cum - mid, EXP_CLAMP))
    kt = k * jnp.exp(jnp.minimum(mid - cum, EXP_CLAMP))
    a = jnp.where(mask, _nt(qt.astype(BF16), kt.astype(BF16)), 0.0)
    o = _nn(a.astype(BF16), v_bf) + _nt((q * jnp.exp(cum)).astype(BF16), st.astype(BF16))
    return st_new, o


def _hgrn_kernel(lb_ref, ng_ref, czf_ref, czb_ref, ci_ref, q_ref, zf_ref, zb_ref, i_ref, g_ref,
                 o_ref, of_scr, ob_scr):
    c = HG_CHUNK
    n_ctx = czf_ref.shape[0] // c
    n_lat = q_ref.shape[0] // c
    row = lax.broadcasted_iota(jnp.int32, (c, c), 0)
    col = lax.broadcasted_iota(jnp.int32, (c, c), 1)
    lower = row >= col
    upper = col >= row
    tri_f = jnp.where(lower, 1.0, 0.0).astype(BF16)
    tri_b = jnp.where(upper, 1.0, 0.0).astype(BF16)
    lb_f = lb_ref[0:1, :]
    lb_b = lb_ref[1:2, :]

    def ctx_body(j, carry):
        sf, sb = carry
        rf = pl.ds(pl.multiple_of(j * c, c), c)
        rb = pl.ds(pl.multiple_of((n_ctx - 1 - j) * c, c), c)
        sf, _ = _hgrn_chunk(sf, czf_ref[rf, :].astype(F32), ci_ref[rf, :], None, lb_f, tri_f, lower, False)
        sb, _ = _hgrn_chunk(sb, czb_ref[rb, :].astype(F32), ci_ref[rb, :], None, lb_b, tri_b, upper, True)
        return sf, sb

    s0 = jnp.zeros((HG_EXPAND, HG_EXPAND), F32)
    sf, sb = lax.fori_loop(0, n_ctx, ctx_body, (s0, s0))

    def lat_body(j, carry):
        sf, sb = carry
        rf = pl.ds(pl.multiple_of(j * c, c), c)
        rb = pl.ds(pl.multiple_of((n_lat - 1 - j) * c, c), c)
        qf = q_ref[rf, :].astype(F32)
        qf = qf * jax.nn.sigmoid(qf)
        qb = q_ref[rb, :].astype(F32)
        qb = qb * jax.nn.sigmoid(qb)
        sf, o_f = _hgrn_chunk(sf, zf_ref[rf, :].astype(F32), i_ref[rf, :], qf, lb_f, tri_f, lower, False)
        sb, o_b = _hgrn_chunk(sb, zb_ref[rb, :].astype(F32), i_ref[rb, :], qb, lb_b, tri_b, upper, True)
        of_scr[rf, :] = o_f
        ob_scr[rb, :] = o_b
        return sf, sb

    lax.fori_loop(0, n_lat, lat_body, (sf, sb))

    rows = 256
    ng = ng_ref[...]

    def out_body(j, _):
        r = pl.ds(pl.multiple_of(j * rows, rows), rows)
        o = of_scr[r, :] + ob_scr[r, :]
        o = o * lax.rsqrt(jnp.mean(o * o, axis=-1, keepdims=True) + RMS_EPS) * ng
        g = g_ref[r, :].astype(F32)
        o_ref[r, :] = (o * (g * jax.nn.sigmoid(g))).astype(BF16)
        return 0

    lax.fori_loop(0, q_ref.shape[0] // rows, out_body, 0)


def _hgrn(cparts, parts, lb, norm_g, n_heads):
    b, t, _ = parts.shape
    tc = cparts.shape[1]
    w = HG_EXPAND

    def col(part):
        return lambda bi, hi: (bi, 0, part * n_heads + hi)

    cspec = lambda part: pl.BlockSpec((None, tc, w), col(part))
    lspec = lambda part: pl.BlockSpec((None, t, w), col(part))
    return pl.pallas_call(
        _hgrn_kernel,
        grid=(b, n_heads),
        in_specs=[pl.BlockSpec((2, w), lambda bi, hi: (0, hi)),
                  pl.BlockSpec((1, w), lambda bi, hi: (0, hi)),
                  cspec(1), cspec(2), cspec(3),
                  lspec(0), lspec(1), lspec(2), lspec(3), lspec(4)],
        out_specs=pl.BlockSpec((None, t, w), lambda bi, hi: (bi, 0, hi)),
        out_shape=jax.ShapeDtypeStruct((b, t, n_heads * w), BF16),
        scratch_shapes=[pltpu.VMEM((t, w), F32), pltpu.VMEM((t, w), F32)],
        compiler_params=_cparams(("parallel", "parallel")),
        name="hgrn2",
    )(lb, norm_g.reshape(1, -1), cparts, cparts, cparts, parts, parts, parts, parts, parts)


def _rope(x, cos, sin):
    lane = lax.broadcasted_iota(jnp.int32, x.shape, 1)
    swapped = jnp.where((lane % 32) < 16,
                        pltpu.roll(x, LANES - 16, axis=1),
                        pltpu.roll(x, 16, axis=1))
    return x * cos + swapped * sin


def _da_kernel(lam_ref, q_ref, kc_ref, kl_ref, vc_ref, vl_ref, cq_ref, sq_ref, ck_ref, sk_ref, g_ref,
               o_ref, k_scr, v_scr):
    tc = kc_ref.shape[0]
    t = kl_ref.shape[0]

    @pl.when(pl.program_id(2) == 0)
    def _():
        k_scr[0:tc, :] = kc_ref[...]
        v_scr[0:tc, :] = vc_ref[...]
        rows = min(512, t)

        def body(j, _):
            r = pl.ds(pl.multiple_of(j * rows, rows), rows)
            ro = pl.ds(pl.multiple_of(tc + j * rows, 128), rows)
            k_scr[ro, :] = _rope(kl_ref[r, :].astype(F32), ck_ref[r, :], sk_ref[r, :]).astype(BF16)
            v_scr[ro, :] = vl_ref[r, :]
            return 0

        lax.fori_loop(0, t // rows, body, 0)

    lam = lam_ref[0]
    out_scale = lam_ref[1]
    qscale = (DA_HEAD_DIM ** -0.5) * math.log2(math.e)
    qr = _rope(q_ref[...].astype(F32), cq_ref[...], sq_ref[...]) * qscale
    lane = lax.broadcasted_iota(jnp.int32, qr.shape, 1)
    q1 = jnp.where(lane < DA_HEAD_DIM, qr, 0.0).astype(BF16)
    q2 = jnp.where(lane >= DA_HEAD_DIM, qr, 0.0).astype(BF16)
    k = k_scr[...]
    s1 = _nt(q1, k)
    s2 = _nt(q2, k)
    p1 = jnp.exp2(s1 - jnp.max(s1, axis=-1, keepdims=True))
    p2 = jnp.exp2(s2 - jnp.max(s2, axis=-1, keepdims=True))
    r1 = 1.0 / jnp.sum(p1, axis=-1, keepdims=True)
    r2 = lam / jnp.sum(p2, axis=-1, keepdims=True)
    p = (p1 * r1 - p2 * r2).astype(BF16)
    o = _nn(p, v_scr[...])
    o = o * lax.rsqrt(jnp.mean(o * o, axis=-1, keepdims=True) + RMS_EPS)
    o_ref[...] = (o * g_ref[...] * out_scale).astype(BF16)


def _diff_attn(cparts, parts, lam2, subln_g, cos_t, sin_t, n_heads, col0, tq):
    b, t, _ = parts.shape
    tc = cparts.shape[1]
    w = LANES

    def col(part):
        return lambda bi, hi, qi: (bi, 0, col0 + part * n_heads + hi)

    return pl.pallas_call(
        _da_kernel,
        grid=(b, n_heads, t // tq),
        in_specs=[pl.BlockSpec(memory_space=pltpu.SMEM),
                  pl.BlockSpec((None, tq, w), lambda bi, hi, qi: (bi, qi, col0 + hi)),
                  pl.BlockSpec((None, tc, w), col(1)),
                  pl.BlockSpec((None, t, w), col(1)),
                  pl.BlockSpec((None, tc, w), col(2)),
                  pl.BlockSpec((None, t, w), col(2)),
                  pl.BlockSpec((tq, w), lambda bi, hi, qi: (qi, 0)),
                  pl.BlockSpec((tq, w), lambda bi, hi, qi: (qi, 0)),
                  pl.BlockSpec((t, w), lambda bi, hi, qi: (0, 0)),
                  pl.BlockSpec((t, w), lambda bi, hi, qi: (0, 0)),
                  pl.BlockSpec((1, w), lambda bi, hi, qi: (0, hi))],
        out_specs=pl.BlockSpec((None, tq, w), lambda bi, hi, qi: (bi, qi, hi)),
        out_shape=jax.ShapeDtypeStruct((b, t, n_heads * w), BF16),
        scratch_shapes=[pltpu.VMEM((tc + t, w), BF16), pltpu.VMEM((tc + t, w), BF16)],
        compiler_params=_cparams(("parallel", "parallel", "arbitrary")),
        name="diff_attn",
    )(lam2, parts, cparts, parts, cparts, parts, cos_t, sin_t, cos_t, sin_t, subln_g.reshape(1, -1))


def _outproj_kernel(hg_ref, da_ref, x_ref, wa_ref, wb_ref, g1_ref, sc_ref, sh_ref, ng_ref, x1_ref, h2_ref):
    mix = _nn(hg_ref[...], wa_ref[...]) + _nn(da_ref[...], wb_ref[...])
    x1 = x_ref[...] + g1_ref[0] * mix
    x1_ref[...] = x1
    y = x1 * lax.rsqrt(jnp.mean(x1 * x1, axis=-1, keepdims=True) + RMS_EPS) * ng_ref[...]
    h2_ref[...] = (y * (1.0 + sc_ref[0]) + sh_ref[0]).astype(BF16)


def _outproj(hg, da, x2, w_out, g1, sc2, sh2, norm_g, rows_per_mod, tm):
    m, d = x2.shape
    half = hg.shape[1]
    bpm = rows_per_mod // tm
    mod = pl.BlockSpec((1, 1, d), lambda i: (i // bpm, 0, 0))
    return pl.pallas_call(
        _outproj_kernel,
        grid=(m // tm,),
        in_specs=[pl.BlockSpec((tm, half), lambda i: (i, 0)),
                  pl.BlockSpec((tm, half), lambda i: (i, 0)),
                  pl.BlockSpec((tm, d), lambda i: (i, 0)),
                  pl.BlockSpec((half, d), lambda i: (0, 0)),
                  pl.BlockSpec((half, d), lambda i: (1, 0)),
                  mod, mod, mod,
                  pl.BlockSpec((1, d), lambda i: (0, 0))],
        out_specs=[pl.BlockSpec((tm, d), lambda i: (i, 0)),
                   pl.BlockSpec((tm, d), lambda i: (i, 0))],
        out_shape=[jax.ShapeDtypeStruct((m, d), F32), jax.ShapeDtypeStruct((m, d), BF16)],
        compiler_params=_cparams(("parallel",)),
        name="outproj",
    )(hg, da, x2, w_out, w_out, g1, sc2, sh2, norm_g)


def _top_values(s, n):
    vals = []
    rank = jnp.full(s.shape, float(n), F32)
    for r in range(n):
        m = jnp.max(s, axis=0, keepdims=True)
        hit = s == m
        vals.append(m)
        rank = jnp.where(hit, float(r), rank)
        s = jnp.where(hit, NEG_INF, s)
    return vals, rank


def _route_kernel(h_ref, wq_ref, sk_ref, n1_ref, a_ref, r2_ref, b_ref):
    kk = PEER_TOPK
    q = _nn(h_ref[...], wq_ref[...]).astype(BF16)
    for h in range(PEER_HEADS):
        s1 = _nt(sk_ref[2 * h], q[:, (2 * h) * LANES:(2 * h + 1) * LANES])
        s2 = _nt(sk_ref[2 * h + 1], q[:, (2 * h + 1) * LANES:(2 * h + 2) * LANES])
        v1, rank1 = _top_values(s1, kk)
        v2, rank2 = _top_values(s2, kk)
        sv2 = jnp.concatenate(v2, axis=0)
        cands = [v1[0] + sv2]
        cands += [v1[r] + sv2[0:8, :] for r in range(1, 8)]
        cands.append(jnp.concatenate(v1[8:16], axis=0) + v2[0])
        top = v1[0] + v2[0]
        z = jnp.zeros_like(top)
        tau = top
        work = cands
        for _ in range(kk):
            m8 = work[1]
            for cnd in work[2:]:
                m8 = jnp.maximum(m8, cnd)
            tau = jnp.maximum(jnp.max(work[0], axis=0, keepdims=True), jnp.max(m8, axis=0, keepdims=True))
            z = z + jnp.exp(tau - top)
            work = [jnp.where(cnd == tau, NEG_INF, cnd) for cnd in work]
        cnt = [jnp.sum(jnp.where(cands[r] >= tau, 1.0, 0.0), axis=0, keepdims=True) for r in range(8)]
        tail = jnp.where(cands[8] >= tau, 1.0, 0.0)
        cnt += [tail[r:r + 1, :] for r in range(8)]
        n1 = jnp.zeros_like(s1)
        for r in range(kk):
            n1 = jnp.where(rank1 == float(r), cnt[r], n1)
        n1_ref[h] = n1
        a_ref[h] = jnp.exp(s1 - v1[0])
        r2_ref[h] = rank2.astype(BF16)
        b_ref[h] = (jnp.exp(s2 - v2[0]) / z).astype(BF16)


def _route(h2, wq, sk, tp):
    p, d = h2.shape
    nq = wq.shape[1]
    tab = lambda: pl.BlockSpec((PEER_HEADS, PEER_NKEYS, tp), lambda i: (0, 0, i))
    return pl.pallas_call(
        _route_kernel,
        grid=(p // tp,),
        in_specs=[pl.BlockSpec((tp, d), lambda i: (i, 0)),
                  pl.BlockSpec((d, nq), lambda i: (0, 0)),
                  pl.BlockSpec((2 * PEER_HEADS, PEER_NKEYS, LANES), lambda i: (0, 0, 0))],
        out_specs=[tab(), tab(), tab(), tab()],
        out_shape=[jax.ShapeDtypeStruct((PEER_HEADS, PEER_NKEYS, p), F32),
                   jax.ShapeDtypeStruct((PEER_HEADS, PEER_NKEYS, p), F32),
                   jax.ShapeDtypeStruct((PEER_HEADS, PEER_NKEYS, p), BF16),
                   jax.ShapeDtypeStruct((PEER_HEADS, PEER_NKEYS, p), BF16)],
        compiler_params=_cparams(("parallel",)),
        name="peer_route",
    )(h2, wq, sk)


PEER_SUB = 512
PEER_LANE_CHUNK = 256


def _peer_kernel(x1_ref, g2_ref, fg_ref, h_ref, n1_ref, a_ref, r2_ref, b_ref, u_ref, v_ref, o_ref,
                 acc, ht_scr, act_scr):
    e = pl.program_id(1)
    te, tp = ht_scr.shape
    n_slab = te // PEER_NKEYS
    slabs_per_sub = PEER_SUB // PEER_NKEYS
    zero = jnp.zeros((), BF16)

    @pl.when(e == 0)
    def _():
        acc[...] = jnp.zeros_like(acc)

    n_sub = te // PEER_SUB

    def first_matmul(sub):
        act_scr[sub % 2] = _nt(u_ref[sub * PEER_SUB:(sub + 1) * PEER_SUB, :], h_ref[...])

    first_matmul(0)
    for sub in range(n_sub):
        if sub + 1 < n_sub:
            first_matmul(sub + 1)
        act = act_scr.at[sub % 2]
        for s in range(slabs_per_sub):
            sl = sub * slabs_per_sub + s
            i1 = e * n_slab + sl
            n1rows = [n1_ref[h, pl.ds(i1, 1), :].astype(BF16) for h in range(PEER_HEADS)]
            arows = [a_ref[h, pl.ds(i1, 1), :].astype(BF16) for h in range(PEER_HEADS)]
            for lc in range(tp // PEER_LANE_CHUNK):
                lanes = slice(lc * PEER_LANE_CHUNK, (lc + 1) * PEER_LANE_CHUNK)
                gate = None
                for h in range(PEER_HEADS):
                    n1row = n1rows[h][:, lanes]
                    arow = arows[h][:, lanes]
                    term = jnp.where(r2_ref[h, :, lanes] < n1row, b_ref[h, :, lanes], zero) * arow
                    gate = term if gate is None else gate + term
                a = act[s * PEER_NKEYS:(s + 1) * PEER_NKEYS, lanes]
                gelu = 0.5 * a * (1.0 + lax.erf(a * math.sqrt(0.5)))
                ht_scr[sl * PEER_NKEYS:(sl + 1) * PEER_NKEYS, lanes] = gate * gelu.astype(BF16)
    acc[...] += _tn(ht_scr[...], v_ref[...])

    @pl.when(e == pl.num_programs(1) - 1)
    def _():
        x = x1_ref[...] + g2_ref[0] * acc[...]
        o_ref[...] = x * lax.rsqrt(jnp.mean(x * x, axis=-1, keepdims=True) + RMS_EPS) * fg_ref[...]


def _peer(x1, g2, fg, h2, n1, a, r2, bb, u_bf, v_bf, rows_per_mod, tp, te):
    p, d = x1.shape
    n_exp = u_bf.shape[0]
    bpm = rows_per_mod // tp
    tab = lambda: pl.BlockSpec((PEER_HEADS, PEER_NKEYS, tp), lambda i, e: (0, 0, i))
    return pl.pallas_call(
        _peer_kernel,
        grid=(p // tp, n_exp // te),
        in_specs=[pl.BlockSpec((tp, d), lambda i, e: (i, 0), pipeline_mode=pl.Buffered(1)),
                  pl.BlockSpec((1, 1, d), lambda i, e: (i // bpm, 0, 0)),
                  pl.BlockSpec((1, d), lambda i, e: (0, 0)),
                  pl.BlockSpec((tp, d), lambda i, e: (i, 0), pipeline_mode=pl.Buffered(1)),
                  tab(), tab(), tab(), tab(),
                  pl.BlockSpec((te, d), lambda i, e: (e, 0)),
                  pl.BlockSpec((te, d), lambda i, e: (e, 0))],
        out_specs=pl.BlockSpec((tp, d), lambda i, e: (i, 0)),
        out_shape=jax.ShapeDtypeStruct((p, d), F32),
        scratch_shapes=[pltpu.VMEM((tp, d), F32),
                        pltpu.VMEM((te, tp), BF16),
                        pltpu.VMEM((2, PEER_SUB, tp), F32)],
        compiler_params=_cparams(("parallel", "arbitrary")),
        name="peer_experts",
    )(x1, g2, fg, h2, n1, a, r2, bb, u_bf, v_bf)


def _rope_tables(t):
    rows = t // GRID_W
    row = jnp.repeat(jnp.arange(rows, dtype=F32), GRID_W)
    colp = jnp.tile(jnp.arange(GRID_W, dtype=F32), rows)
    n_freq = DA_HEAD_DIM // 4
    inv_freq = ROPE_BASE ** (-jnp.arange(n_freq, dtype=F32) / n_freq)
    ar = row[:, None] * inv_freq[None, :]
    ac = colp[:, None] * inv_freq[None, :]
    cos64 = jnp.concatenate([jnp.cos(ar), jnp.cos(ar), jnp.cos(ac), jnp.cos(ac)], axis=-1)
    sin64 = jnp.concatenate([-jnp.sin(ar), jnp.sin(ar), -jnp.sin(ac), jnp.sin(ac)], axis=-1)
    return jnp.tile(cos64, (1, 2)), jnp.tile(sin64, (1, 2))


def _layer(x, c, ctx, c_ctx, w_ada, b_ada, norm1_g, w_in, lb, hg_norm_g, lam, lam_init, da_subln_g,
           w_out, norm2_g, peer_w_query, peer_sub_keys, peer_u, peer_v, final_norm_g):
    b, t, d = x.shape
    tc = ctx.shape[1]
    hg_width = hg_norm_g.shape[0]
    hg_heads = hg_width // HG_EXPAND
    da_heads = da_subln_g.shape[0] // DA_V_DIM

    cond = jnp.zeros((8, d), F32).at[0:b].set(c).at[b].set(c_ctx)
    mod = _adaln(cond, w_ada, b_ada)
    sh1, sc1, g1, sh2, sc2, g2 = [m[0:b].reshape(b, 1, d) for m in jnp.split(mod, 6, axis=-1)]
    csh1, csc1 = [m[b:b + 1].reshape(1, 1, d) for m in jnp.split(mod, 6, axis=-1)[0:2]]

    w_in_bf = w_in.astype(BF16)
    n_cols = w_in.shape[1]
    g1n = norm1_g.reshape(1, d)
    tm_lat = min(1024, t)
    parts = _inproj(x.reshape(b * t, d), sc1, sh1, g1n, w_in_bf, t, tm_lat, 1024).reshape(b, t, n_cols)
    tm_ctx = min(512, b * tc)
    cparts = _inproj(ctx.reshape(b * tc, d), csc1, csh1, g1n, w_in_bf, b * tc, tm_ctx, 1024).reshape(b, tc, n_cols)

    hg = _hgrn(cparts, parts, lb, hg_norm_g, hg_heads)
    cos_t, sin_t = _rope_tables(t)
    lam2 = jnp.stack([lam, jnp.asarray(1.0 - lam_init, F32)]).astype(F32)
    da = _diff_attn(cparts, parts, lam2, da_subln_g, cos_t, sin_t, da_heads, 5 * hg_heads, min(256, t))

    x2 = x.reshape(b * t, d)
    x1, h2 = _outproj(hg.reshape(b * t, -1), da.reshape(b * t, -1), x2, w_out.astype(BF16),
                      g1, sc2, sh2, norm2_g.reshape(1, d), t, min(512, t))

    sk = peer_sub_keys.reshape(2 * PEER_HEADS, PEER_NKEYS, -1).astype(BF16)
    n1, a, r2, bb = _route(h2, peer_w_query.astype(BF16), sk, min(256, t))
    out = _peer(x1, g2, final_norm_g.reshape(1, d), h2, n1, a, r2, bb,
                peer_u.astype(BF16), peer_v.astype(BF16), t, min(512, t), 1024)
    return out.reshape(b, t, d)


def kernel(x, c, ctx, c_ctx, w_ada, b_ada, norm1_g, w_in, hg_gamma, hg_norm_g, da_lambda_q1, da_lambda_k1,
           da_lambda_q2, da_lambda_k2, da_subln_g, w_out, norm2_g, peer_w_query, peer_sub_keys, peer_u,
           peer_v, final_norm_g):
    depth = w_ada.shape[0]
    assert depth == 1, "single-layer block: the context stream is never updated"
    lb_all = jnp.cumsum(jax.nn.softmax(hg_gamma.astype(F32), axis=0), axis=0)
    l = 0
    lam_init = 0.8 - 0.6 * math.exp(-0.3 * l)
    lam = (jnp.exp(jnp.sum(da_lambda_q1[l].astype(F32) * da_lambda_k1[l].astype(F32)))
           - jnp.exp(jnp.sum(da_lambda_q2[l].astype(F32) * da_lambda_k2[l].astype(F32))) + lam_init)
    return _layer(x, c, ctx, c_ctx, w_ada[l], b_ada[l], norm1_g[l], w_in[l], lb_all[l], hg_norm_g[l],
                  lam, lam_init, da_subln_g[l], w_out[l], norm2_g[l], peer_w_query[l], peer_sub_keys[l],
                  peer_u[l], peer_v[l], final_norm_g)
```

```python
import functools
import math

import jax
import jax.numpy as jnp
from jax import lax
from jax.experimental import pallas as pl
from jax.experimental.pallas import tpu as pltpu

F32 = jnp.float32
BF16 = jnp.bfloat16

GRID_W = 64
HG_EXPAND = 128
HG_CHUNK = 64
HG_UNROLL = 2
DA_HEAD_DIM = 64
DA_V_DIM = 128
DA_Q_SUB = 256
ROPE_BASE = 10000.0
PEER_HEADS = 8
PEER_NKEYS = 128
PEER_TOPK = 16
RMS_EPS = 1e-6
LANES = 128
EXP_CLAMP = 80.0
VMEM_LIMIT = 56 * 1024 * 1024
NEG_INF = float("-inf")


def _nt(a, b):
    return lax.dot_general(a, b, (((1,), (1,)), ((), ())), preferred_element_type=F32)


def _tn(a, b):
    return lax.dot_general(a, b, (((0,), (0,)), ((), ())), preferred_element_type=F32)


def _nn(a, b):
    return jnp.dot(a, b, preferred_element_type=F32)


def _cparams(sem):
    return pltpu.CompilerParams(dimension_semantics=sem, vmem_limit_bytes=VMEM_LIMIT)


def _adaln_kernel(c_ref, w_ref, b_ref, o_ref):
    c = c_ref[...]
    a = (c * jax.nn.sigmoid(c)).astype(BF16)
    o_ref[...] = _nn(a, w_ref[...].astype(BF16)) + b_ref[...]


def _adaln(cond, w, b):
    d, n = w.shape
    tn = 1024
    return pl.pallas_call(
        _adaln_kernel,
        grid=(n // tn,),
        in_specs=[pl.BlockSpec((8, d), lambda j: (0, 0)),
                  pl.BlockSpec((d, tn), lambda j: (0, j)),
                  pl.BlockSpec((1, tn), lambda j: (0, j))],
        out_specs=pl.BlockSpec((8, tn), lambda j: (0, j)),
        out_shape=jax.ShapeDtypeStruct((8, n), F32),
        compiler_params=_cparams(("arbitrary",)),
        name="adaln",
    )(cond, w, b.reshape(1, n))


def _inproj_kernel(x_ref, sc_ref, sh_ref, g_ref, w_ref, o_ref, h_scr):
    @pl.when(pl.program_id(1) == 0)
    def _():
        x = x_ref[...]
        y = x * lax.rsqrt(jnp.mean(x * x, axis=-1, keepdims=True) + RMS_EPS) * g_ref[...]
        h_scr[...] = (y * (1.0 + sc_ref[0]) + sh_ref[0]).astype(BF16)

    o_ref[...] = _nn(h_scr[...], w_ref[...]).astype(BF16)


def _inproj(x2, sc, sh, g, w, rows_per_mod, tm, tn):
    m, d = x2.shape
    n = w.shape[1]
    bpm = rows_per_mod // tm
    return pl.pallas_call(
        _inproj_kernel,
        grid=(m // tm, n // tn),
        in_specs=[pl.BlockSpec((tm, d), lambda i, j: (i, 0)),
                  pl.BlockSpec((1, 1, d), lambda i, j: (i // bpm, 0, 0)),
                  pl.BlockSpec((1, 1, d), lambda i, j: (i // bpm, 0, 0)),
                  pl.BlockSpec((1, d), lambda i, j: (0, 0)),
                  pl.BlockSpec((d, tn), lambda i, j: (0, j))],
        out_specs=pl.BlockSpec((tm, tn), lambda i, j: (i, j)),
        out_shape=jax.ShapeDtypeStruct((m, n), BF16),
        scratch_shapes=[pltpu.VMEM((tm, d), BF16)],
        compiler_params=_cparams(("parallel", "arbitrary")),
        name="inproj",
    )(x2, sc, sh, g, w)


def _hgrn_gates(z, lb, tri):
    f = lb + (1.0 - lb) * jax.nn.sigmoid(z)
    lf = jnp.log(f)
    hi = lf.astype(BF16)
    lo = (lf - hi.astype(F32)).astype(BF16)
    return 1.0 - f, _nn(tri, hi) + _nn(tri, lo)


def _hgrn_chunk(st, z, v_bf, q, lb, tri, mask, rev):
    c = HG_CHUNK
    k, cum = _hgrn_gates(z, lb, tri)
    tot = cum[0:1, :] if rev else cum[c - 1:c, :]
    k_st = k * jnp.exp(tot - cum)
    st_new = st * jnp.exp(tot) + _tn(v_bf, k_st.astype(BF16))
    if q is None:
        return st_new, None, None, None
    mid = cum[c // 2:c // 2 + 1, :] if rev else cum[c // 2 - 1:c // 2, :]
    rel = cum - mid
    qt = q * jnp.exp(jnp.minimum(rel, EXP_CLAMP))
    kt = k * jnp.exp(jnp.minimum(-rel, EXP_CLAMP))
    a = jnp.where(mask, _nt(qt.astype(BF16), kt.astype(BF16)), 0.0)
    intra = _nn(a.astype(BF16), v_bf)
    inter = _nt((q * jnp.exp(cum)).astype(BF16), st.astype(BF16))
    return st_new, intra, inter, jnp.abs(rel)


def _hgrn_intra_exact(z, v_bf, q, lb, tri, mask):
    c = HG_CHUNK
    k, cum = _hgrn_gates(z, lb, tri)
    col = lax.broadcasted_iota(jnp.int32, (c, c), 1)
    a = jnp.zeros((c, c), F32)
    for s in range(c):
        d = jnp.exp(jnp.minimum(cum - cum[s:s + 1, :], 0.0))
        a = jnp.where(col == s, jnp.sum(q * d * k[s:s + 1, :], axis=-1, keepdims=True), a)
    return _nn(jnp.where(mask, a, 0.0).astype(BF16), v_bf)


def _silu(x):
    return x * jax.nn.sigmoid(x)


def _hgrn_kernel(lb_ref, ng_ref, czf_ref, czb_ref, ci_ref, q_ref, zf_ref, zb_ref, i_ref, g_ref,
                 o_ref, of_scr, ob_scr):
    c = HG_CHUNK
    un = HG_UNROLL
    n_ctx = czf_ref.shape[0] // c
    n_lat = q_ref.shape[0] // c
    row = lax.broadcasted_iota(jnp.int32, (c, c), 0)
    col = lax.broadcasted_iota(jnp.int32, (c, c), 1)
    lower = row >= col
    upper = col >= row
    tri_f = jnp.where(lower, 1.0, 0.0).astype(BF16)
    tri_b = jnp.where(upper, 1.0, 0.0).astype(BF16)
    lb_f = lb_ref[0:1, :]
    lb_b = lb_ref[1:2, :]

    def rows_of(chunk):
        return pl.ds(pl.multiple_of(chunk * c, c), c)

    def ctx_body(j, carry):
        sf, sb = carry
        for u in range(un):
            rf = rows_of(j * un + u)
            rb = rows_of(n_ctx - 1 - (j * un + u))
            sf = _hgrn_chunk(sf, czf_ref[rf, :].astype(F32), ci_ref[rf, :], None, lb_f, tri_f, lower, False)[0]
            sb = _hgrn_chunk(sb, czb_ref[rb, :].astype(F32), ci_ref[rb, :], None, lb_b, tri_b, upper, True)[0]
        return sf, sb

    s0 = jnp.zeros((HG_EXPAND, HG_EXPAND), F32)
    sf, sb = lax.fori_loop(0, n_ctx // un, ctx_body, (s0, s0))

    def lat_body(j, carry):
        sf, sb = carry
        worst = None
        redo = []
        for u in range(un):
            rf = rows_of(j * un + u)
            rb = rows_of(n_lat - 1 - (j * un + u))
            sf, intra_f, inter_f, ex_f = _hgrn_chunk(sf, zf_ref[rf, :].astype(F32), i_ref[rf, :],
                                                     _silu(q_ref[rf, :].astype(F32)), lb_f, tri_f, lower, False)
            sb, intra_b, inter_b, ex_b = _hgrn_chunk(sb, zb_ref[rb, :].astype(F32), i_ref[rb, :],
                                                     _silu(q_ref[rb, :].astype(F32)), lb_b, tri_b, upper, True)
            of_scr[rf, :] = intra_f + inter_f
            ob_scr[rb, :] = intra_b + inter_b
            ex = jnp.maximum(ex_f, ex_b)
            worst = ex if worst is None else jnp.maximum(worst, ex)
            redo.append((rf, rb, inter_f, inter_b))

        @pl.when(jnp.max(worst) > EXP_CLAMP)
        def _():
            for rf, rb, inter_f, inter_b in redo:
                of_scr[rf, :] = inter_f + _hgrn_intra_exact(
                    zf_ref[rf, :].astype(F32), i_ref[rf, :], _silu(q_ref[rf, :].astype(F32)), lb_f, tri_f, lower)
                ob_scr[rb, :] = inter_b + _hgrn_intra_exact(
                    zb_ref[rb, :].astype(F32), i_ref[rb, :], _silu(q_ref[rb, :].astype(F32)), lb_b, tri_b, upper)

        return sf, sb

    lax.fori_loop(0, n_lat // un, lat_body, (sf, sb))

    rows = 256
    ng = ng_ref[...]

    def out_body(j, _):
        r = pl.ds(pl.multiple_of(j * rows, rows), rows)
        o = of_scr[r, :] + ob_scr[r, :]
        o = o * lax.rsqrt(jnp.mean(o * o, axis=-1, keepdims=True) + RMS_EPS) * ng
        o_ref[r, :] = (o * _silu(g_ref[r, :].astype(F32))).astype(BF16)
        return 0

    lax.fori_loop(0, q_ref.shape[0] // rows, out_body, 0)


def _hgrn(cparts, parts, lb, norm_g, n_heads):
    b, t, _ = parts.shape
    tc = cparts.shape[1]
    w = HG_EXPAND

    def col(part):
        return lambda bi, hi: (bi, 0, part * n_heads + hi)

    cspec = lambda part: pl.BlockSpec((None, tc, w), col(part))
    lspec = lambda part: pl.BlockSpec((None, t, w), col(part))
    return pl.pallas_call(
        _hgrn_kernel,
        grid=(b, n_heads),
        in_specs=[pl.BlockSpec((2, w), lambda bi, hi: (0, hi)),
                  pl.BlockSpec((1, w), lambda bi, hi: (0, hi)),
                  cspec(1), cspec(2), cspec(3),
                  lspec(0), lspec(1), lspec(2), lspec(3), lspec(4)],
        out_specs=pl.BlockSpec((None, t, w), lambda bi, hi: (bi, 0, hi)),
        out_shape=jax.ShapeDtypeStruct((b, t, n_heads * w), BF16),
        scratch_shapes=[pltpu.VMEM((t, w), F32), pltpu.VMEM((t, w), F32)],
        compiler_params=_cparams(("parallel", "parallel")),
        name="hgrn2",
    )(lb, norm_g.reshape(1, -1), cparts, cparts, cparts, parts, parts, parts, parts, parts)


def _rope(x, cos, sin):
    lane = lax.broadcasted_iota(jnp.int32, x.shape, 1)
    swapped = jnp.where((lane % 32) < 16,
                        pltpu.roll(x, LANES - 16, axis=1),
                        pltpu.roll(x, 16, axis=1))
    return x * cos + swapped * sin


def _da_kernel(lam_ref, q_ref, kc_ref, kl_ref, vc_ref, vl_ref, cq_ref, sq_ref, ck_ref, sk_ref, g_ref,
               o_ref, k_scr, v_scr):
    tc = kc_ref.shape[0]
    t = kl_ref.shape[0]
    dv = DA_V_DIM

    @pl.when(pl.program_id(2) == 0)
    def _():
        lane = lax.broadcasted_iota(jnp.int32, (tc + t, LANES), 1)
        v_scr[:, dv:dv + LANES] = jnp.where(lane == 0, 1.0, 0.0).astype(BF16)
        k_scr[:, 0:tc] = kc_ref[...].astype(F32).T.astype(BF16)
        v_scr[0:tc, 0:dv] = vc_ref[...]
        v_scr[tc:tc + t, 0:dv] = vl_ref[...]
        rows = min(512, t)
        for j in range(t // rows):
            r = slice(j * rows, (j + 1) * rows)
            kr = _rope(kl_ref[r, :].astype(F32), ck_ref[r, :], sk_ref[r, :])
            k_scr[:, tc + j * rows:tc + (j + 1) * rows] = kr.T.astype(BF16)

    lam = lam_ref[0]
    out_scale = lam_ref[1]
    qscale = (DA_HEAD_DIM ** -0.5) * math.log2(math.e)
    k = k_scr[...]
    v1 = v_scr[...]

    def branch(qm):
        s = _nn(qm.astype(BF16), k)
        p = jnp.exp2(s - jnp.max(s, axis=-1, keepdims=True)).astype(BF16)
        ov = _nn(p, v1)
        return ov[:, 0:dv] / ov[:, dv:dv + 1]

    sub = min(DA_Q_SUB, q_ref.shape[0])
    for j in range(q_ref.shape[0] // sub):
        r = slice(j * sub, (j + 1) * sub)
        qr = _rope(q_ref[r, :].astype(F32), cq_ref[r, :], sq_ref[r, :]) * qscale
        lane = lax.broadcasted_iota(jnp.int32, qr.shape, 1)
        o = (branch(jnp.where(lane < DA_HEAD_DIM, qr, 0.0))
             - lam * branch(jnp.where(lane >= DA_HEAD_DIM, qr, 0.0)))
        o = o * lax.rsqrt(jnp.mean(o * o, axis=-1, keepdims=True) + RMS_EPS)
        o_ref[r, :] = (o * g_ref[...] * out_scale).astype(BF16)


def _diff_attn(cparts, parts, lam2, subln_g, cos_t, sin_t, n_heads, col0, tq):
    b, t, _ = parts.shape
    tc = cparts.shape[1]
    w = LANES

    def col(part):
        return lambda bi, hi, qi: (bi, 0, col0 + part * n_heads + hi)

    return pl.pallas_call(
        _da_kernel,
        grid=(b, n_heads, t // tq),
        in_specs=[pl.BlockSpec(memory_space=pltpu.SMEM),
                  pl.BlockSpec((None, tq, w), lambda bi, hi, qi: (bi, qi, col0 + hi)),
                  pl.BlockSpec((None, tc, w), col(1)),
                  pl.BlockSpec((None, t, w), col(1)),
                  pl.BlockSpec((None, tc, w), col(2)),
                  pl.BlockSpec((None, t, w), col(2)),
                  pl.BlockSpec((tq, w), lambda bi, hi, qi: (qi, 0)),
                  pl.BlockSpec((tq, w), lambda bi, hi, qi: (qi, 0)),
                  pl.BlockSpec((t, w), lambda bi, hi, qi: (0, 0)),
                  pl.BlockSpec((t, w), lambda bi, hi, qi: (0, 0)),
                  pl.BlockSpec((1, w), lambda bi, hi, qi: (0, hi))],
        out_specs=pl.BlockSpec((None, tq, w), lambda bi, hi, qi: (bi, qi, hi)),
        out_shape=jax.ShapeDtypeStruct((b, t, n_heads * w), BF16),
        scratch_shapes=[pltpu.VMEM((w, tc + t), BF16), pltpu.VMEM((tc + t, DA_V_DIM + w), BF16)],
        compiler_params=_cparams(("parallel", "parallel", "arbitrary")),
        name="diff_attn",
    )(lam2, parts, cparts, parts, cparts, parts, cos_t, sin_t, cos_t, sin_t, subln_g.reshape(1, -1))


def _outproj_kernel(hg_ref, da_ref, x_ref, wa_ref, wb_ref, g1_ref, sc_ref, sh_ref, ng_ref, x1_ref, h2_ref):
    mix = _nn(hg_ref[...], wa_ref[...]) + _nn(da_ref[...], wb_ref[...])
    x1 = x_ref[...] + g1_ref[0] * mix
    x1_ref[...] = x1
    y = x1 * lax.rsqrt(jnp.mean(x1 * x1, axis=-1, keepdims=True) + RMS_EPS) * ng_ref[...]
    h2_ref[...] = (y * (1.0 + sc_ref[0]) + sh_ref[0]).astype(BF16)


def _outproj(hg, da, x2, w_out, g1, sc2, sh2, norm_g, rows_per_mod, tm):
    m, d = x2.shape
    half = hg.shape[1]
    bpm = rows_per_mod // tm
    mod = pl.BlockSpec((1, 1, d), lambda i: (i // bpm, 0, 0))
    return pl.pallas_call(
        _outproj_kernel,
        grid=(m // tm,),
        in_specs=[pl.BlockSpec((tm, half), lambda i: (i, 0)),
                  pl.BlockSpec((tm, half), lambda i: (i, 0)),
                  pl.BlockSpec((tm, d), lambda i: (i, 0)),
                  pl.BlockSpec((half, d), lambda i: (0, 0)),
                  pl.BlockSpec((half, d), lambda i: (1, 0)),
                  mod, mod, mod,
                  pl.BlockSpec((1, d), lambda i: (0, 0))],
        out_specs=[pl.BlockSpec((tm, d), lambda i: (i, 0)),
                   pl.BlockSpec((tm, d), lambda i: (i, 0))],
        out_shape=[jax.ShapeDtypeStruct((m, d), F32), jax.ShapeDtypeStruct((m, d), BF16)],
        compiler_params=_cparams(("parallel",)),
        name="outproj",
    )(hg, da, x2, w_out, w_out, g1, sc2, sh2, norm_g)


def _top_values(s, n):
    vals = []
    rank = jnp.full(s.shape, float(n), F32)
    for r in range(n):
        m = jnp.max(s, axis=0, keepdims=True)
        hit = s == m
        vals.append(m)
        rank = jnp.where(hit, float(r), rank)
        s = jnp.where(hit, NEG_INF, s)
    return vals, rank


def _route_kernel(h_ref, wq_ref, sk_ref, n1_ref, a_ref, r2_ref, b_ref):
    kk = PEER_TOPK
    q = _nn(h_ref[...], wq_ref[...]).astype(BF16)
    for h in range(PEER_HEADS):
        s1 = _nt(sk_ref[2 * h], q[:, (2 * h) * LANES:(2 * h + 1) * LANES])
        s2 = _nt(sk_ref[2 * h + 1], q[:, (2 * h + 1) * LANES:(2 * h + 2) * LANES])
        v1, rank1 = _top_values(s1, kk)
        v2, rank2 = _top_values(s2, kk)
        sv2 = jnp.concatenate(v2, axis=0)
        cands = [v1[0] + sv2]
        cands += [v1[r] + sv2[0:8, :] for r in range(1, 8)]
        cands.append(jnp.concatenate(v1[8:16], axis=0) + v2[0])
        top = v1[0] + v2[0]
        z = jnp.zeros_like(top)
        tau = top
        work = cands
        for _ in range(kk):
            m8 = work[1]
            for cnd in work[2:]:
                m8 = jnp.maximum(m8, cnd)
            tau = jnp.maximum(jnp.max(work[0], axis=0, keepdims=True), jnp.max(m8, axis=0, keepdims=True))
            z = z + jnp.exp(tau - top)
            work = [jnp.where(cnd == tau, NEG_INF, cnd) for cnd in work]
        cnt = [jnp.sum(jnp.where(cands[r] >= tau, 1.0, 0.0), axis=0, keepdims=True) for r in range(8)]
        tail = jnp.where(cands[8] >= tau, 1.0, 0.0)
        cnt += [tail[r:r + 1, :] for r in range(8)]
        n1 = jnp.zeros_like(s1)
        for r in range(kk):
            n1 = jnp.where(rank1 == float(r), cnt[r], n1)
        n1_ref[h] = n1
        a_ref[h] = jnp.exp(s1 - v1[0])
        r2_ref[h] = rank2.astype(BF16)
        b_ref[h] = (jnp.exp(s2 - v2[0]) / z).astype(BF16)


def _route(h2, wq, sk, tp):
    p, d = h2.shape
    nq = wq.shape[1]
    tab = lambda: pl.BlockSpec((PEER_HEADS, PEER_NKEYS, tp), lambda i: (0, 0, i))
    return pl.pallas_call(
        _route_kernel,
        grid=(p // tp,),
        in_specs=[pl.BlockSpec((tp, d), lambda i: (i, 0)),
                  pl.BlockSpec((d, nq), lambda i: (0, 0)),
                  pl.BlockSpec((2 * PEER_HEADS, PEER_NKEYS, LANES), lambda i: (0, 0, 0))],
        out_specs=[tab(), tab(), tab(), tab()],
        out_shape=[jax.ShapeDtypeStruct((PEER_HEADS, PEER_NKEYS, p), F32),
                   jax.ShapeDtypeStruct((PEER_HEADS, PEER_NKEYS, p), F32),
                   jax.ShapeDtypeStruct((PEER_HEADS, PEER_NKEYS, p), BF16),
                   jax.ShapeDtypeStruct((PEER_HEADS, PEER_NKEYS, p), BF16)],
        compiler_params=_cparams(("parallel",)),
        name="peer_route",
    )(h2, wq, sk)


PEER_SUB = 512
PEER_LANE_CHUNK = 256


def _peer_kernel(x1_ref, g2_ref, fg_ref, h_ref, n1_ref, a_ref, r2_ref, b_ref, u_ref, v_ref, o_ref,
                 acc, ht_scr, act_scr):
    e = pl.program_id(1)
    te, tp = ht_scr.shape
    n_slab = te // PEER_NKEYS
    slabs_per_sub = PEER_SUB // PEER_NKEYS
    zero = jnp.zeros((), BF16)

    @pl.when(e == 0)
    def _():
        acc[...] = jnp.zeros_like(acc)

    n_sub = te // PEER_SUB

    def first_matmul(sub):
        act_scr[sub % 2] = _nt(u_ref[sub * PEER_SUB:(sub + 1) * PEER_SUB, :], h_ref[...])

    first_matmul(0)
    for sub in range(n_sub):
        if sub + 1 < n_sub:
            first_matmul(sub + 1)
        act = act_scr.at[sub % 2]
        for s in range(slabs_per_sub):
            sl = sub * slabs_per_sub + s
            i1 = e * n_slab + sl
            n1rows = [n1_ref[h, pl.ds(i1, 1), :].astype(BF16) for h in range(PEER_HEADS)]
            arows = [a_ref[h, pl.ds(i1, 1), :].astype(BF16) for h in range(PEER_HEADS)]
            for lc in range(tp // PEER_LANE_CHUNK):
                lanes = slice(lc * PEER_LANE_CHUNK, (lc + 1) * PEER_LANE_CHUNK)
                gate = None
                for h in range(PEER_HEADS):
                    n1row = n1rows[h][:, lanes]
                    arow = arows[h][:, lanes]
                    term = jnp.where(r2_ref[h, :, lanes] < n1row, b_ref[h, :, lanes], zero) * arow
                    gate = term if gate is None else gate + term
                a = act[s * PEER_NKEYS:(s + 1) * PEER_NKEYS, lanes]
                gelu = 0.5 * a * (1.0 + lax.erf(a * math.sqrt(0.5)))
                ht_scr[sl * PEER_NKEYS:(sl + 1) * PEER_NKEYS, lanes] = gate * gelu.astype(BF16)
    acc[...] += _tn(ht_scr[...], v_ref[...])

    @pl.when(e == pl.num_programs(1) - 1)
    def _():
        x = x1_ref[...] + g2_ref[0] * acc[...]
        o_ref[...] = x * lax.rsqrt(jnp.mean(x * x, axis=-1, keepdims=True) + RMS_EPS) * fg_ref[...]


def _peer(x1, g2, fg, h2, n1, a, r2, bb, u_bf, v_bf, rows_per_mod, tp, te):
    p, d = x1.shape
    n_exp = u_bf.shape[0]
    bpm = rows_per_mod // tp
    tab = lambda: pl.BlockSpec((PEER_HEADS, PEER_NKEYS, tp), lambda i, e: (0, 0, i))
    return pl.pallas_call(
        _peer_kernel,
        grid=(p // tp, n_exp // te),
        in_specs=[pl.BlockSpec((tp, d), lambda i, e: (i, 0), pipeline_mode=pl.Buffered(1)),
                  pl.BlockSpec((1, 1, d), lambda i, e: (i // bpm, 0, 0)),
                  pl.BlockSpec((1, d), lambda i, e: (0, 0)),
                  pl.BlockSpec((tp, d), lambda i, e: (i, 0), pipeline_mode=pl.Buffered(1)),
                  tab(), tab(), tab(), tab(),
                  pl.BlockSpec((te, d), lambda i, e: (e, 0)),
                  pl.BlockSpec((te, d), lambda i, e: (e, 0))],
        out_specs=pl.BlockSpec((tp, d), lambda i, e: (i, 0)),
        out_shape=jax.ShapeDtypeStruct((p, d), F32),
        scratch_shapes=[pltpu.VMEM((tp, d), F32),
                        pltpu.VMEM((te, tp), BF16),
                        pltpu.VMEM((2, PEER_SUB, tp), F32)],
        compiler_params=_cparams(("parallel", "arbitrary")),
        name="peer_experts",
    )(x1, g2, fg, h2, n1, a, r2, bb, u_bf, v_bf)


def _rope_tables(t):
    rows = t // GRID_W
    row = jnp.repeat(jnp.arange(rows, dtype=F32), GRID_W)
    colp = jnp.tile(jnp.arange(GRID_W, dtype=F32), rows)
    n_freq = DA_HEAD_DIM // 4
    inv_freq = ROPE_BASE ** (-jnp.arange(n_freq, dtype=F32) / n_freq)
    ar = row[:, None] * inv_freq[None, :]
    ac = colp[:, None] * inv_freq[None, :]
    cos64 = jnp.concatenate([jnp.cos(ar), jnp.cos(ar), jnp.cos(ac), jnp.cos(ac)], axis=-1)
    sin64 = jnp.concatenate([-jnp.sin(ar), jnp.sin(ar), -jnp.sin(ac), jnp.sin(ac)], axis=-1)
    return jnp.tile(cos64, (1, 2)), jnp.tile(sin64, (1, 2))


def _layer(x, c, ctx, c_ctx, w_ada, b_ada, norm1_g, w_in, lb, hg_norm_g, lam, lam_init, da_subln_g,
           w_out, norm2_g, peer_w_query, peer_sub_keys, peer_u, peer_v, final_norm_g):
    b, t, d = x.shape
    tc = ctx.shape[1]
    hg_width = hg_norm_g.shape[0]
    hg_heads = hg_width // HG_EXPAND
    da_heads = da_subln_g.shape[0] // DA_V_DIM

    cond = jnp.zeros((8, d), F32).at[0:b].set(c).at[b].set(c_ctx)
    mod = _adaln(cond, w_ada, b_ada)
    sh1, sc1, g1, sh2, sc2, g2 = [m[0:b].reshape(b, 1, d) for m in jnp.split(mod, 6, axis=-1)]
    csh1, csc1 = [m[b:b + 1].reshape(1, 1, d) for m in jnp.split(mod, 6, axis=-1)[0:2]]

    w_in_bf = w_in.astype(BF16)
    n_cols = w_in.shape[1]
    g1n = norm1_g.reshape(1, d)
    tm_lat = min(1024, t)
    parts = _inproj(x.reshape(b * t, d), sc1, sh1, g1n, w_in_bf, t, tm_lat, 1024).reshape(b, t, n_cols)
    tm_ctx = min(512, b * tc)
    cparts = _inproj(ctx.reshape(b * tc, d), csc1, csh1, g1n, w_in_bf, b * tc, tm_ctx, 1024).reshape(b, tc, n_cols)

    hg = _hgrn(cparts, parts, lb, hg_norm_g, hg_heads)
    cos_t, sin_t = _rope_tables(t)
    lam2 = jnp.stack([lam, jnp.asarray(1.0 - lam_init, F32)]).astype(F32)
    da = _diff_attn(cparts, parts, lam2, da_subln_g, cos_t, sin_t, da_heads, 5 * hg_heads, min(1024, t))

    x2 = x.reshape(b * t, d)
    x1, h2 = _outproj(hg.reshape(b * t, -1), da.reshape(b * t, -1), x2, w_out.astype(BF16),
                      g1, sc2, sh2, norm2_g.reshape(1, d), t, min(512, t))

    sk = peer_sub_keys.reshape(2 * PEER_HEADS, PEER_NKEYS, -1).astype(BF16)
    n1, a, r2, bb = _route(h2, peer_w_query.astype(BF16), sk, min(256, t))
    out = _peer(x1, g2, final_norm_g.reshape(1, d), h2, n1, a, r2, bb,
                peer_u.astype(BF16), peer_v.astype(BF16), t, min(512, t), 1024)
    return out.reshape(b, t, d)


def kernel(x, c, ctx, c_ctx, w_ada, b_ada, norm1_g, w_in, hg_gamma, hg_norm_g, da_lambda_q1, da_lambda_k1,
           da_lambda_q2, da_lambda_k2, da_subln_g, w_out, norm2_g, peer_w_query, peer_sub_keys, peer_u,
           peer_v, final_norm_g):
    depth = w_ada.shape[0]
    assert depth == 1, "single-layer block: the context stream is never updated"
    lb_all = jnp.cumsum(jax.nn.softmax(hg_gamma.astype(F32), axis=0), axis=0)
    l = 0
    lam_init = 0.8 - 0.6 * math.exp(-0.3 * l)
    lam = (jnp.exp(jnp.sum(da_lambda_q1[l].astype(F32) * da_lambda_k1[l].astype(F32)))
           - jnp.exp(jnp.sum(da_lambda_q2[l].astype(F32) * da_lambda_k2[l].astype(F32))) + lam_init)
    return _layer(x, c, ctx, c_ctx, w_ada[l], b_ada[l], norm1_g[l], w_in[l], lb_all[l], hg_norm_g[l],
                  lam, lam_init, da_subln_g[l], w_out[l], norm2_g[l], peer_w_query[l], peer_sub_keys[l],
                  peer_u[l], peer_v[l], final_norm_g)
```

```python
import functools
import math

import jax
import jax.numpy as jnp
from jax import lax
from jax.experimental import pallas as pl
from jax.experimental.pallas import tpu as pltpu

F32 = jnp.float32
BF16 = jnp.bfloat16

GRID_W = 64
HG_EXPAND = 128
HG_CHUNK = 64
HG_UNROLL = 2
DA_HEAD_DIM = 64
DA_V_DIM = 128
DA_Q_SUB = 256
ROPE_BASE = 10000.0
PEER_HEADS = 8
PEER_NKEYS = 128
PEER_TOPK = 16
RMS_EPS = 1e-6
LANES = 128
EXP_CLAMP = 80.0
VMEM_LIMIT = 56 * 1024 * 1024
NEG_INF = float("-inf")


def _nt(a, b):
    return lax.dot_general(a, b, (((1,), (1,)), ((), ())), preferred_element_type=F32)


def _tn(a, b):
    return lax.dot_general(a, b, (((0,), (0,)), ((), ())), preferred_element_type=F32)


def _nn(a, b):
    return jnp.dot(a, b, preferred_element_type=F32)


def _cparams(sem):
    return pltpu.CompilerParams(dimension_semantics=sem, vmem_limit_bytes=VMEM_LIMIT)


def _adaln_kernel(c_ref, w_ref, b_ref, o_ref):
    c = c_ref[...]
    a = (c * jax.nn.sigmoid(c)).astype(BF16)
    o_ref[...] = _nn(a, w_ref[...].astype(BF16)) + b_ref[...]


def _adaln(cond, w, b):
    d, n = w.shape
    tn = 1024
    return pl.pallas_call(
        _adaln_kernel,
        grid=(n // tn,),
        in_specs=[pl.BlockSpec((8, d), lambda j: (0, 0)),
                  pl.BlockSpec((d, tn), lambda j: (0, j)),
                  pl.BlockSpec((1, tn), lambda j: (0, j))],
        out_specs=pl.BlockSpec((8, tn), lambda j: (0, j)),
        out_shape=jax.ShapeDtypeStruct((8, n), F32),
        compiler_params=_cparams(("arbitrary",)),
        name="adaln",
    )(cond, w, b.reshape(1, n))


def _inproj_kernel(x_ref, sc_ref, sh_ref, g_ref, w_ref, o_ref, h_scr):
    @pl.when(pl.program_id(1) == 0)
    def _():
        x = x_ref[...]
        y = x * lax.rsqrt(jnp.mean(x * x, axis=-1, keepdims=True) + RMS_EPS) * g_ref[...]
        h_scr[...] = (y * (1.0 + sc_ref[0]) + sh_ref[0]).astype(BF16)

    o_ref[...] = _nn(h_scr[...], w_ref[...]).astype(BF16)


def _inproj(x2, sc, sh, g, w, rows_per_mod, tm, tn):
    m, d = x2.shape
    n = w.shape[1]
    bpm = rows_per_mod // tm
    return pl.pallas_call(
        _inproj_kernel,
        grid=(m // tm, n // tn),
        in_specs=[pl.BlockSpec((tm, d), lambda i, j: (i, 0)),
                  pl.BlockSpec((1, 1, d), lambda i, j: (i // bpm, 0, 0)),
                  pl.BlockSpec((1, 1, d), lambda i, j: (i // bpm, 0, 0)),
                  pl.BlockSpec((1, d), lambda i, j: (0, 0)),
                  pl.BlockSpec((d, tn), lambda i, j: (0, j))],
        out_specs=pl.BlockSpec((tm, tn), lambda i, j: (i, j)),
        out_shape=jax.ShapeDtypeStruct((m, n), BF16),
        scratch_shapes=[pltpu.VMEM((tm, d), BF16)],
        compiler_params=_cparams(("parallel", "arbitrary")),
        name="inproj",
    )(x2, sc, sh, g, w)


def _hgrn_gates(z, lb, tri):
    f = lb + (1.0 - lb) * jax.nn.sigmoid(z)
    lf = jnp.log(f)
    hi = lf.astype(BF16)
    lo = (lf - hi.astype(F32)).astype(BF16)
    both = _nn(tri, jnp.concatenate([hi, lo], axis=1))
    w = lf.shape[1]
    return 1.0 - f, both[:, 0:w] + both[:, w:2 * w]


def _hgrn_masks(rev):
    n = 2 * HG_CHUNK
    row = lax.broadcasted_iota(jnp.int32, (n, n), 0)
    col = lax.broadcasted_iota(jnp.int32, (n, n), 1)
    same = (row >= HG_CHUNK) == (col >= HG_CHUNK)
    before = (col >= row) if rev else (row >= col)
    diag = same & before
    cross = (~same) & before
    return jnp.where(diag, 1.0, 0.0).astype(BF16), diag, cross


def _hgrn_block(st, z, v_bf, q, lb, masks, rev):
    c = HG_CHUNK
    tri, diag, cross = masks
    k, cum = _hgrn_gates(z, lb, tri)
    second = lax.broadcasted_iota(jnp.int32, cum.shape, 0) < c if rev else \
        lax.broadcasted_iota(jnp.int32, cum.shape, 0) >= c
    tot_lo = cum[0:1, :] if rev else cum[c - 1:c, :]
    tot_hi = cum[c:c + 1, :] if rev else cum[2 * c - 1:2 * c, :]
    tot_first, tot_second = (tot_hi, tot_lo) if rev else (tot_lo, tot_hi)
    tot_own = jnp.where(lax.broadcasted_iota(jnp.int32, cum.shape, 0) < c, tot_lo, tot_hi)
    k_st = k * jnp.exp(tot_own - cum)
    k_end = k_st * jnp.where(second, 1.0, jnp.exp(tot_second))
    st_new = st * jnp.exp(tot_first + tot_second) + _tn(v_bf, k_end.astype(BF16))
    if q is None:
        return st_new, None, None, None
    mid_lo = cum[c // 2:c // 2 + 1, :] if rev else cum[c // 2 - 1:c // 2, :]
    mid_hi = cum[c + c // 2:c + c // 2 + 1, :] if rev else cum[c + c // 2 - 1:c + c // 2, :]
    rel = cum - jnp.where(lax.broadcasted_iota(jnp.int32, cum.shape, 0) < c, mid_lo, mid_hi)
    qt = q * jnp.exp(jnp.minimum(rel, EXP_CLAMP))
    kt = k * jnp.exp(jnp.minimum(-rel, EXP_CLAMP))
    q_in = q * jnp.exp(cum)
    scores = (jnp.where(diag, _nt(qt.astype(BF16), kt.astype(BF16)), 0.0)
              + jnp.where(cross, _nt(q_in.astype(BF16), k_st.astype(BF16)), 0.0))
    intra = _nn(scores.astype(BF16), v_bf)
    q_start = q_in * jnp.where(second, jnp.exp(tot_first), 1.0)
    inter = _nt(q_start.astype(BF16), st.astype(BF16))
    return st_new, intra, inter, jnp.abs(rel)


def _hgrn_intra_exact(z, v_bf, q, lb, masks, rev, cum_scr, k_scr):
    c = HG_CHUNK
    n = 2 * c
    tri, diag, cross = masks
    k, cum = _hgrn_gates(z, lb, tri)
    rows = lax.broadcasted_iota(jnp.int32, cum.shape, 0)
    if rev:
        cum = cum + jnp.where(rows < c, cum[c:c + 1, :], 0.0)
    else:
        cum = cum + jnp.where(rows >= c, cum[c - 1:c, :], 0.0)
    col = lax.broadcasted_iota(jnp.int32, (n, n), 1)
    cum_scr[...] = cum
    k_scr[...] = k

    def key_column(s, a):
        d = jnp.exp(jnp.minimum(cum - cum_scr[pl.ds(s, 1), :], 0.0))
        return jnp.where(col == s, jnp.sum(q * d * k_scr[pl.ds(s, 1), :], axis=-1, keepdims=True), a)

    a = lax.fori_loop(0, n, key_column, jnp.zeros((n, n), F32))
    return _nn(jnp.where(diag | cross, a, 0.0).astype(BF16), v_bf)


def _silu(x):
    return x * jax.nn.sigmoid(x)


def _hgrn_kernel(lb_ref, ng_ref, czf_ref, czb_ref, ci_ref, q_ref, zf_ref, zb_ref, i_ref, g_ref,
                 o_ref, of_scr, ob_scr, cum_scr, k_scr):
    n = 2 * HG_CHUNK
    un = HG_UNROLL
    n_ctx = czf_ref.shape[0] // n
    n_lat = q_ref.shape[0] // n
    masks_f = _hgrn_masks(False)
    masks_b = _hgrn_masks(True)
    lb_f = lb_ref[0:1, :]
    lb_b = lb_ref[1:2, :]

    def rows_of(block):
        return pl.ds(pl.multiple_of(block * n, n), n)

    def ctx_body(j, carry):
        sf, sb = carry
        rf = rows_of(j)
        rb = rows_of(n_ctx - 1 - j)
        sf = _hgrn_block(sf, czf_ref[rf, :].astype(F32), ci_ref[rf, :], None, lb_f, masks_f, False)[0]
        sb = _hgrn_block(sb, czb_ref[rb, :].astype(F32), ci_ref[rb, :], None, lb_b, masks_b, True)[0]
        return sf, sb

    s0 = jnp.zeros((HG_EXPAND, HG_EXPAND), F32)
    sf, sb = lax.fori_loop(0, n_ctx, ctx_body, (s0, s0))

    def lat_body(j, carry):
        sf, sb = carry
        worst = None
        redo = []
        for u in range(un):
            rf = rows_of(j * un + u)
            rb = rows_of(n_lat - 1 - (j * un + u))
            sf, intra_f, inter_f, ex_f = _hgrn_block(sf, zf_ref[rf, :].astype(F32), i_ref[rf, :],
                                                     _silu(q_ref[rf, :].astype(F32)), lb_f, masks_f, False)
            sb, intra_b, inter_b, ex_b = _hgrn_block(sb, zb_ref[rb, :].astype(F32), i_ref[rb, :],
                                                     _silu(q_ref[rb, :].astype(F32)), lb_b, masks_b, True)
            of_scr[rf, :] = intra_f + inter_f
            ob_scr[rb, :] = intra_b + inter_b
            ex = jnp.maximum(ex_f, ex_b)
            worst = ex if worst is None else jnp.maximum(worst, ex)
            redo.append((rf, rb, inter_f, inter_b))

        @pl.when(jnp.max(worst) > EXP_CLAMP)
        def _():
            for rf, rb, inter_f, inter_b in redo:
                of_scr[rf, :] = inter_f + _hgrn_intra_exact(
                    zf_ref[rf, :].astype(F32), i_ref[rf, :], _silu(q_ref[rf, :].astype(F32)), lb_f, masks_f, False,
                    cum_scr, k_scr)
                ob_scr[rb, :] = inter_b + _hgrn_intra_exact(
                    zb_ref[rb, :].astype(F32), i_ref[rb, :], _silu(q_ref[rb, :].astype(F32)), lb_b, masks_b, True,
                    cum_scr, k_scr)

        return sf, sb

    lax.fori_loop(0, n_lat // un, lat_body, (sf, sb))

    rows = 256
    ng = ng_ref[...]

    def out_body(j, _):
        r = pl.ds(pl.multiple_of(j * rows, rows), rows)
        o = of_scr[r, :] + ob_scr[r, :]
        o = o * lax.rsqrt(jnp.mean(o * o, axis=-1, keepdims=True) + RMS_EPS) * ng
        o_ref[r, :] = (o * _silu(g_ref[r, :].astype(F32))).astype(BF16)
        return 0

    lax.fori_loop(0, q_ref.shape[0] // rows, out_body, 0)


def _hgrn(cparts, parts, lb, norm_g, n_heads):
    b, t, _ = parts.shape
    tc = cparts.shape[1]
    w = HG_EXPAND

    def col(part):
        return lambda bi, hi: (bi, 0, part * n_heads + hi)

    cspec = lambda part: pl.BlockSpec((None, tc, w), col(part))
    lspec = lambda part: pl.BlockSpec((None, t, w), col(part))
    return pl.pallas_call(
        _hgrn_kernel,
        grid=(b, n_heads),
        in_specs=[pl.BlockSpec((2, w), lambda bi, hi: (0, hi)),
                  pl.BlockSpec((1, w), lambda bi, hi: (0, hi)),
                  cspec(1), cspec(2), cspec(3),
                  lspec(0), lspec(1), lspec(2), lspec(3), lspec(4)],
        out_specs=pl.BlockSpec((None, t, w), lambda bi, hi: (bi, 0, hi)),
        out_shape=jax.ShapeDtypeStruct((b, t, n_heads * w), BF16),
        scratch_shapes=[pltpu.VMEM((t, w), F32), pltpu.VMEM((t, w), F32),
                        pltpu.VMEM((2 * HG_CHUNK, w), F32), pltpu.VMEM((2 * HG_CHUNK, w), F32)],
        compiler_params=_cparams(("parallel", "parallel")),
        name="hgrn2",
    )(lb, norm_g.reshape(1, -1), cparts, cparts, cparts, parts, parts, parts, parts, parts)


def _rope(x, cos, sin):
    lane = lax.broadcasted_iota(jnp.int32, x.shape, 1)
    swapped = jnp.where((lane % 32) < 16,
                        pltpu.roll(x, LANES - 16, axis=1),
                        pltpu.roll(x, 16, axis=1))
    return x * cos + swapped * sin


def _da_kernel(lam_ref, q_ref, kc_ref, kl_ref, vc_ref, vl_ref, cq_ref, sq_ref, ck_ref, sk_ref, g_ref,
               o_ref, k_scr, v_scr):
    tc = kc_ref.shape[0]
    t = kl_ref.shape[0]
    dv = DA_V_DIM

    @pl.when(pl.program_id(2) == 0)
    def _():
        lane = lax.broadcasted_iota(jnp.int32, (tc + t, LANES), 1)
        v_scr[:, dv:dv + LANES] = jnp.where(lane == 0, 1.0, 0.0).astype(BF16)
        k_scr[:, 0:tc] = kc_ref[...].astype(F32).T.astype(BF16)
        v_scr[0:tc, 0:dv] = vc_ref[...]
        v_scr[tc:tc + t, 0:dv] = vl_ref[...]
        rows = min(512, t)
        for j in range(t // rows):
            r = slice(j * rows, (j + 1) * rows)
            kr = _rope(kl_ref[r, :].astype(F32), ck_ref[r, :], sk_ref[r, :])
            k_scr[:, tc + j * rows:tc + (j + 1) * rows] = kr.T.astype(BF16)

    lam = lam_ref[0]
    out_scale = lam_ref[1]
    qscale = (DA_HEAD_DIM ** -0.5) * math.log2(math.e)
    k = k_scr[...]
    v1 = v_scr[...]

    def branch(qm):
        s = _nn(qm.astype(BF16), k)
        p = jnp.exp2(s - jnp.max(s, axis=-1, keepdims=True)).astype(BF16)
        ov = _nn(p, v1)
        return ov[:, 0:dv] / ov[:, dv:dv + 1]

    sub = min(DA_Q_SUB, q_ref.shape[0])
    for j in range(q_ref.shape[0] // sub):
        r = slice(j * sub, (j + 1) * sub)
        qr = _rope(q_ref[r, :].astype(F32), cq_ref[r, :], sq_ref[r, :]) * qscale
        lane = lax.broadcasted_iota(jnp.int32, qr.shape, 1)
        o = (branch(jnp.where(lane < DA_HEAD_DIM, qr, 0.0))
             - lam * branch(jnp.where(lane >= DA_HEAD_DIM, qr, 0.0)))
        o = o * lax.rsqrt(jnp.mean(o * o, axis=-1, keepdims=True) + RMS_EPS)
        o_ref[r, :] = (o * g_ref[...] * out_scale).astype(BF16)


def _diff_attn(cparts, parts, lam2, subln_g, cos_t, sin_t, n_heads, col0, tq):
    b, t, _ = parts.shape
    tc = cparts.shape[1]
    w = LANES

    def col(part):
        return lambda bi, hi, qi: (bi, 0, col0 + part * n_heads + hi)

    return pl.pallas_call(
        _da_kernel,
        grid=(b, n_heads, t // tq),
        in_specs=[pl.BlockSpec(memory_space=pltpu.SMEM),
                  pl.BlockSpec((None, tq, w), lambda bi, hi, qi: (bi, qi, col0 + hi)),
                  pl.BlockSpec((None, tc, w), col(1)),
                  pl.BlockSpec((None, t, w), col(1)),
                  pl.BlockSpec((None, tc, w), col(2)),
                  pl.BlockSpec((None, t, w), col(2)),
                  pl.BlockSpec((tq, w), lambda bi, hi, qi: (qi, 0)),
                  pl.BlockSpec((tq, w), lambda bi, hi, qi: (qi, 0)),
                  pl.BlockSpec((t, w), lambda bi, hi, qi: (0, 0)),
                  pl.BlockSpec((t, w), lambda bi, hi, qi: (0, 0)),
                  pl.BlockSpec((1, w), lambda bi, hi, qi: (0, hi))],
        out_specs=pl.BlockSpec((None, tq, w), lambda bi, hi, qi: (bi, qi, hi)),
        out_shape=jax.ShapeDtypeStruct((b, t, n_heads * w), BF16),
        scratch_shapes=[pltpu.VMEM((w, tc + t), BF16), pltpu.VMEM((tc + t, DA_V_DIM + w), BF16)],
        compiler_params=_cparams(("parallel", "parallel", "arbitrary")),
        name="diff_attn",
    )(lam2, parts, cparts, parts, cparts, parts, cos_t, sin_t, cos_t, sin_t, subln_g.reshape(1, -1))


def _outproj_kernel(hg_ref, da_ref, x_ref, wa_ref, wb_ref, g1_ref, sc_ref, sh_ref, ng_ref, x1_ref, h2_ref):
    mix = _nn(hg_ref[...], wa_ref[...]) + _nn(da_ref[...], wb_ref[...])
    x1 = x_ref[...] + g1_ref[0] * mix
    x1_ref[...] = x1
    y = x1 * lax.rsqrt(jnp.mean(x1 * x1, axis=-1, keepdims=True) + RMS_EPS) * ng_ref[...]
    h2_ref[...] = (y * (1.0 + sc_ref[0]) + sh_ref[0]).astype(BF16)


def _outproj(hg, da, x2, w_out, g1, sc2, sh2, norm_g, rows_per_mod, tm):
    m, d = x2.shape
    half = hg.shape[1]
    bpm = rows_per_mod // tm
    mod = pl.BlockSpec((1, 1, d), lambda i: (i // bpm, 0, 0))
    return pl.pallas_call(
        _outproj_kernel,
        grid=(m // tm,),
        in_specs=[pl.BlockSpec((tm, half), lambda i: (i, 0)),
                  pl.BlockSpec((tm, half), lambda i: (i, 0)),
                  pl.BlockSpec((tm, d), lambda i: (i, 0)),
                  pl.BlockSpec((half, d), lambda i: (0, 0)),
                  pl.BlockSpec((half, d), lambda i: (1, 0)),
                  mod, mod, mod,
                  pl.BlockSpec((1, d), lambda i: (0, 0))],
        out_specs=[pl.BlockSpec((tm, d), lambda i: (i, 0)),
                   pl.BlockSpec((tm, d), lambda i: (i, 0))],
        out_shape=[jax.ShapeDtypeStruct((m, d), F32), jax.ShapeDtypeStruct((m, d), BF16)],
        compiler_params=_cparams(("parallel",)),
        name="outproj",
    )(hg, da, x2, w_out, w_out, g1, sc2, sh2, norm_g)


def _top_values(s, n):
    vals = []
    rank = jnp.full(s.shape, float(n), F32)
    for r in range(n):
        m = jnp.max(s, axis=0, keepdims=True)
        hit = s == m
        vals.append(m)
        rank = jnp.where(hit, float(r), rank)
        s = jnp.where(hit, NEG_INF, s)
    return vals, rank


def _route_kernel(h_ref, wq_ref, sk_ref, n1_ref, a_ref, r2_ref, b_ref):
    kk = PEER_TOPK
    q = _nn(h_ref[...], wq_ref[...]).astype(BF16)
    for h in range(PEER_HEADS):
        s1 = _nt(sk_ref[2 * h], q[:, (2 * h) * LANES:(2 * h + 1) * LANES])
        s2 = _nt(sk_ref[2 * h + 1], q[:, (2 * h + 1) * LANES:(2 * h + 2) * LANES])
        v1, rank1 = _top_values(s1, kk)
        v2, rank2 = _top_values(s2, kk)
        sv2 = jnp.concatenate(v2, axis=0)
        cands = [v1[0] + sv2]
        cands += [v1[r] + sv2[0:8, :] for r in range(1, 8)]
        cands.append(jnp.concatenate(v1[8:16], axis=0) + v2[0])
        top = v1[0] + v2[0]
        z = jnp.zeros_like(top)
        tau = top
        work = cands
        for _ in range(kk):
            m8 = work[1]
            for cnd in work[2:]:
                m8 = jnp.maximum(m8, cnd)
            tau = jnp.maximum(jnp.max(work[0], axis=0, keepdims=True), jnp.max(m8, axis=0, keepdims=True))
            z = z + jnp.exp(tau - top)
            work = [jnp.where(cnd == tau, NEG_INF, cnd) for cnd in work]
        cnt = [jnp.sum(jnp.where(cands[r] >= tau, 1.0, 0.0), axis=0, keepdims=True) for r in range(8)]
        tail = jnp.where(cands[8] >= tau, 1.0, 0.0)
        cnt += [tail[r:r + 1, :] for r in range(8)]
        n1 = jnp.zeros_like(s1)
        for r in range(kk):
            n1 = jnp.where(rank1 == float(r), cnt[r], n1)
        n1_ref[h] = n1
        a_ref[h] = jnp.exp(s1 - v1[0])
        r2_ref[h] = rank2.astype(BF16)
        b_ref[h] = (jnp.exp(s2 - v2[0]) / z).astype(BF16)


def _route(h2, wq, sk, tp):
    p, d = h2.shape
    nq = wq.shape[1]
    tab = lambda: pl.BlockSpec((PEER_HEADS, PEER_NKEYS, tp), lambda i: (0, 0, i))
    return pl.pallas_call(
        _route_kernel,
        grid=(p // tp,),
        in_specs=[pl.BlockSpec((tp, d), lambda i: (i, 0)),
                  pl.BlockSpec((d, nq), lambda i: (0, 0)),
                  pl.BlockSpec((2 * PEER_HEADS, PEER_NKEYS, LANES), lambda i: (0, 0, 0))],
        out_specs=[tab(), tab(), tab(), tab()],
        out_shape=[jax.ShapeDtypeStruct((PEER_HEADS, PEER_NKEYS, p), F32),
                   jax.ShapeDtypeStruct((PEER_HEADS, PEER_NKEYS, p), F32),
                   jax.ShapeDtypeStruct((PEER_HEADS, PEER_NKEYS, p), BF16),
                   jax.ShapeDtypeStruct((PEER_HEADS, PEER_NKEYS, p), BF16)],
        compiler_params=_cparams(("parallel",)),
        name="peer_route",
    )(h2, wq, sk)


PEER_SUB = 512
PEER_LANE_CHUNK = 256


def _peer_kernel(x1_ref, g2_ref, fg_ref, h_ref, n1_ref, a_ref, r2_ref, b_ref, u_ref, v_ref, o_ref,
                 acc, ht_scr, act_a, act_b):
    e = pl.program_id(1)
    te, tp = ht_scr.shape
    n_slab = te // PEER_NKEYS
    slabs_per_sub = PEER_SUB // PEER_NKEYS
    zero = jnp.zeros((), BF16)

    @pl.when(e == 0)
    def _():
        acc[...] = jnp.zeros_like(acc)

    n_sub = te // PEER_SUB
    act_bufs = (act_a, act_b)

    def first_matmul(sub):
        act_bufs[sub % 2][...] = _nt(u_ref[sub * PEER_SUB:(sub + 1) * PEER_SUB, :], h_ref[...])

    first_matmul(0)
    for sub in range(n_sub):
        if sub + 1 < n_sub:
            first_matmul(sub + 1)
        act = act_bufs[sub % 2]
        for s in range(slabs_per_sub):
            sl = sub * slabs_per_sub + s
            i1 = e * n_slab + sl
            n1rows = [n1_ref[h, pl.ds(i1, 1), :].astype(BF16) for h in range(PEER_HEADS)]
            arows = [a_ref[h, pl.ds(i1, 1), :].astype(BF16) for h in range(PEER_HEADS)]
            for lc in range(tp // PEER_LANE_CHUNK):
                lanes = slice(lc * PEER_LANE_CHUNK, (lc + 1) * PEER_LANE_CHUNK)
                gate = None
                for h in range(PEER_HEADS):
                    n1row = n1rows[h][:, lanes]
                    arow = arows[h][:, lanes]
                    term = jnp.where(r2_ref[h, :, lanes] < n1row, b_ref[h, :, lanes], zero) * arow
                    gate = term if gate is None else gate + term
                a = act[s * PEER_NKEYS:(s + 1) * PEER_NKEYS, lanes]
                gelu = 0.5 * a * (1.0 + lax.erf(a * math.sqrt(0.5)))
                ht_scr[sl * PEER_NKEYS:(sl + 1) * PEER_NKEYS, lanes] = gate * gelu.astype(BF16)
    acc[...] += _tn(ht_scr[...], v_ref[...])

    @pl.when(e == pl.num_programs(1) - 1)
    def _():
        x = x1_ref[...] + g2_ref[0] * acc[...]
        o_ref[...] = x * lax.rsqrt(jnp.mean(x * x, axis=-1, keepdims=True) + RMS_EPS) * fg_ref[...]


def _peer(x1, g2, fg, h2, n1, a, r2, bb, u_bf, v_bf, rows_per_mod, tp, te):
    p, d = x1.shape
    n_exp = u_bf.shape[0]
    bpm = rows_per_mod // tp
    tab = lambda: pl.BlockSpec((PEER_HEADS, PEER_NKEYS, tp), lambda i, e: (0, 0, i))
    return pl.pallas_call(
        _peer_kernel,
        grid=(p // tp, n_exp // te),
        in_specs=[pl.BlockSpec((tp, d), lambda i, e: (i, 0), pipeline_mode=pl.Buffered(1)),
                  pl.BlockSpec((1, 1, d), lambda i, e: (i // bpm, 0, 0)),
                  pl.BlockSpec((1, d), lambda i, e: (0, 0)),
                  pl.BlockSpec((tp, d), lambda i, e: (i, 0), pipeline_mode=pl.Buffered(1)),
                  tab(), tab(), tab(), tab(),
                  pl.BlockSpec((te, d), lambda i, e: (e, 0)),
                  pl.BlockSpec((te, d), lambda i, e: (e, 0))],
        out_specs=pl.BlockSpec((tp, d), lambda i, e: (i, 0)),
        out_shape=jax.ShapeDtypeStruct((p, d), F32),
        scratch_shapes=[pltpu.VMEM((tp, d), F32),
                        pltpu.VMEM((te, tp), BF16),
                        pltpu.VMEM((PEER_SUB, tp), F32),
                        pltpu.VMEM((PEER_SUB, tp), F32)],
        compiler_params=_cparams(("parallel", "arbitrary")),
        name="peer_experts",
    )(x1, g2, fg, h2, n1, a, r2, bb, u_bf, v_bf)


def _rope_tables(t):
    rows = t // GRID_W
    row = jnp.repeat(jnp.arange(rows, dtype=F32), GRID_W)
    colp = jnp.tile(jnp.arange(GRID_W, dtype=F32), rows)
    n_freq = DA_HEAD_DIM // 4
    inv_freq = ROPE_BASE ** (-jnp.arange(n_freq, dtype=F32) / n_freq)
    ar = row[:, None] * inv_freq[None, :]
    ac = colp[:, None] * inv_freq[None, :]
    cos64 = jnp.concatenate([jnp.cos(ar), jnp.cos(ar), jnp.cos(ac), jnp.cos(ac)], axis=-1)
    sin64 = jnp.concatenate([-jnp.sin(ar), jnp.sin(ar), -jnp.sin(ac), jnp.sin(ac)], axis=-1)
    return jnp.tile(cos64, (1, 2)), jnp.tile(sin64, (1, 2))


def _layer(x, c, ctx, c_ctx, w_ada, b_ada, norm1_g, w_in, lb, hg_norm_g, lam, lam_init, da_subln_g,
           w_out, norm2_g, peer_w_query, peer_sub_keys, peer_u, peer_v, final_norm_g):
    b, t, d = x.shape
    tc = ctx.shape[1]
    hg_width = hg_norm_g.shape[0]
    hg_heads = hg_width // HG_EXPAND
    da_heads = da_subln_g.shape[0] // DA_V_DIM

    cond = jnp.zeros((8, d), F32).at[0:b].set(c).at[b].set(c_ctx)
    mod = _adaln(cond, w_ada, b_ada)
    sh1, sc1, g1, sh2, sc2, g2 = [m[0:b].reshape(b, 1, d) for m in jnp.split(mod, 6, axis=-1)]
    csh1, csc1 = [m[b:b + 1].reshape(1, 1, d) for m in jnp.split(mod, 6, axis=-1)[0:2]]

    w_in_bf = w_in.astype(BF16)
    n_cols = w_in.shape[1]
    g1n = norm1_g.reshape(1, d)
    tm_lat = min(1024, t)
    parts = _inproj(x.reshape(b * t, d), sc1, sh1, g1n, w_in_bf, t, tm_lat, 1024).reshape(b, t, n_cols)
    tm_ctx = min(512, b * tc)
    cparts = _inproj(ctx.reshape(b * tc, d), csc1, csh1, g1n, w_in_bf, b * tc, tm_ctx, 1024).reshape(b, tc, n_cols)

    hg = _hgrn(cparts, parts, lb, hg_norm_g, hg_heads)
    cos_t, sin_t = _rope_tables(t)
    lam2 = jnp.stack([lam, jnp.asarray(1.0 - lam_init, F32)]).astype(F32)
    da = _diff_attn(cparts, parts, lam2, da_subln_g, cos_t, sin_t, da_heads, 5 * hg_heads, min(1024, t))

    x2 = x.reshape(b * t, d)
    x1, h2 = _outproj(hg.reshape(b * t, -1), da.reshape(b * t, -1), x2, w_out.astype(BF16),
                      g1, sc2, sh2, norm2_g.reshape(1, d), t, min(512, t))

    sk = peer_sub_keys.reshape(2 * PEER_HEADS, PEER_NKEYS, -1).astype(BF16)
    n1, a, r2, bb = _route(h2, peer_w_query.astype(BF16), sk, min(256, t))
    out = _peer(x1, g2, final_norm_g.reshape(1, d), h2, n1, a, r2, bb,
                peer_u.astype(BF16), peer_v.astype(BF16), t, min(512, t), 1024)
    return out.reshape(b, t, d)


def kernel(x, c, ctx, c_ctx, w_ada, b_ada, norm1_g, w_in, hg_gamma, hg_norm_g, da_lambda_q1, da_lambda_k1,
           da_lambda_q2, da_lambda_k2, da_subln_g, w_out, norm2_g, peer_w_query, peer_sub_keys, peer_u,
           peer_v, final_norm_g):
    depth = w_ada.shape[0]
    assert depth == 1, "single-layer block: the context stream is never updated"
    lb_all = jnp.cumsum(jax.nn.softmax(hg_gamma.astype(F32), axis=0), axis=0)
    l = 0
    lam_init = 0.8 - 0.6 * math.exp(-0.3 * l)
    lam = (jnp.exp(jnp.sum(da_lambda_q1[l].astype(F32) * da_lambda_k1[l].astype(F32)))
           - jnp.exp(jnp.sum(da_lambda_q2[l].astype(F32) * da_lambda_k2[l].astype(F32))) + lam_init)
    return _layer(x, c, ctx, c_ctx, w_ada[l], b_ada[l], norm1_g[l], w_in[l], lb_all[l], hg_norm_g[l],
                  lam, lam_init, da_subln_g[l], w_out[l], norm2_g[l], peer_w_query[l], peer_sub_keys[l],
                  peer_u[l], peer_v[l], final_norm_g)
```

```python
import functools
import math

import jax
import jax.numpy as jnp
from jax import lax
from jax.experimental import pallas as pl
from jax.experimental.pallas import tpu as pltpu

F32 = jnp.float32
BF16 = jnp.bfloat16

GRID_W = 64
HG_EXPAND = 128
HG_CHUNK = 64
HG_UNROLL = 4
DA_HEAD_DIM = 64
DA_V_DIM = 128
DA_Q_SUB = 256
ROPE_BASE = 10000.0
PEER_HEADS = 8
PEER_NKEYS = 128
PEER_TOPK = 16
RMS_EPS = 1e-6
LANES = 128
BF16_SUBLANES = 16
EXP_CLAMP = 80.0
VMEM_LIMIT = 56 * 1024 * 1024
NEG_INF = float("-inf")


def _nt(a, b):
    return lax.dot_general(a, b, (((1,), (1,)), ((), ())), preferred_element_type=F32)


def _tn(a, b):
    return lax.dot_general(a, b, (((0,), (0,)), ((), ())), preferred_element_type=F32)


def _nn(a, b):
    return jnp.dot(a, b, preferred_element_type=F32)


def _cparams(sem):
    return pltpu.CompilerParams(dimension_semantics=sem, vmem_limit_bytes=VMEM_LIMIT)


def _adaln_kernel(c_ref, w_ref, b_ref, o_ref):
    c = c_ref[...]
    a = (c * jax.nn.sigmoid(c)).astype(BF16)
    o_ref[...] = _nn(a, w_ref[...].astype(BF16)) + b_ref[...]


def _adaln(cond, w, b):
    d, n = w.shape
    tn = 1024
    return pl.pallas_call(
        _adaln_kernel,
        grid=(n // tn,),
        in_specs=[pl.BlockSpec((8, d), lambda j: (0, 0)),
                  pl.BlockSpec((d, tn), lambda j: (0, j)),
                  pl.BlockSpec((1, tn), lambda j: (0, j))],
        out_specs=pl.BlockSpec((8, tn), lambda j: (0, j)),
        out_shape=jax.ShapeDtypeStruct((8, n), F32),
        compiler_params=_cparams(("arbitrary",)),
        name="adaln",
    )(cond, w, b.reshape(1, n))


def _inproj_kernel(x_ref, sc_ref, sh_ref, g_ref, w_ref, o_ref, h_scr):
    @pl.when(pl.program_id(1) == 0)
    def _():
        x = x_ref[...]
        y = x * lax.rsqrt(jnp.mean(x * x, axis=-1, keepdims=True) + RMS_EPS) * g_ref[...]
        h_scr[...] = (y * (1.0 + sc_ref[0]) + sh_ref[0]).astype(BF16)

    o_ref[...] = _nn(h_scr[...], w_ref[...]).astype(BF16)


def _inproj(x2, sc, sh, g, w, rows_per_mod, tm, tn):
    m, d = x2.shape
    n = w.shape[1]
    bpm = rows_per_mod // tm
    return pl.pallas_call(
        _inproj_kernel,
        grid=(m // tm, n // tn),
        in_specs=[pl.BlockSpec((tm, d), lambda i, j: (i, 0)),
                  pl.BlockSpec((1, 1, d), lambda i, j: (i // bpm, 0, 0)),
                  pl.BlockSpec((1, 1, d), lambda i, j: (i // bpm, 0, 0)),
                  pl.BlockSpec((1, d), lambda i, j: (0, 0)),
                  pl.BlockSpec((d, tn), lambda i, j: (0, j))],
        out_specs=pl.BlockSpec((tm, tn), lambda i, j: (i, j)),
        out_shape=jax.ShapeDtypeStruct((m, n), BF16),
        scratch_shapes=[pltpu.VMEM((tm, d), BF16)],
        compiler_params=_cparams(("parallel", "arbitrary")),
        name="inproj",
    )(x2, sc, sh, g, w)


def _hgrn_gates(z, lb, tri):
    f = lb + (1.0 - lb) * jax.nn.sigmoid(z)
    lf = jnp.log(f)
    hi = lf.astype(BF16)
    lo = (lf - hi.astype(F32)).astype(BF16)
    both = _nn(tri, jnp.concatenate([hi, lo], axis=1))
    w = lf.shape[1]
    return 1.0 - f, both[:, 0:w] + both[:, w:2 * w]


def _hgrn_masks(rev):
    n = 2 * HG_CHUNK
    row = lax.broadcasted_iota(jnp.int32, (n, n), 0)
    col = lax.broadcasted_iota(jnp.int32, (n, n), 1)
    same = (row >= HG_CHUNK) == (col >= HG_CHUNK)
    before = (col >= row) if rev else (row >= col)
    diag = same & before
    cross = (~same) & before
    return jnp.where(diag, 1.0, 0.0).astype(BF16), diag, cross


def _hgrn_block(st, z, v_bf, q, lb, masks, rev):
    c = HG_CHUNK
    tri, diag, cross = masks
    k, cum = _hgrn_gates(z, lb, tri)
    second = lax.broadcasted_iota(jnp.int32, cum.shape, 0) < c if rev else \
        lax.broadcasted_iota(jnp.int32, cum.shape, 0) >= c
    tot_lo = cum[0:1, :] if rev else cum[c - 1:c, :]
    tot_hi = cum[c:c + 1, :] if rev else cum[2 * c - 1:2 * c, :]
    tot_first, tot_second = (tot_hi, tot_lo) if rev else (tot_lo, tot_hi)
    tot_own = jnp.where(lax.broadcasted_iota(jnp.int32, cum.shape, 0) < c, tot_lo, tot_hi)
    k_st = k * jnp.exp(tot_own - cum)
    k_end = k_st * jnp.where(second, 1.0, jnp.exp(tot_second))
    st_new = st * jnp.exp(tot_first + tot_second) + _tn(v_bf, k_end.astype(BF16))
    if q is None:
        return st_new, None, None, None
    mid_lo = cum[c // 2:c // 2 + 1, :] if rev else cum[c // 2 - 1:c // 2, :]
    mid_hi = cum[c + c // 2:c + c // 2 + 1, :] if rev else cum[c + c // 2 - 1:c + c // 2, :]
    rel = cum - jnp.where(lax.broadcasted_iota(jnp.int32, cum.shape, 0) < c, mid_lo, mid_hi)
    qt = q * jnp.exp(jnp.minimum(rel, EXP_CLAMP))
    kt = k * jnp.exp(jnp.minimum(-rel, EXP_CLAMP))
    q_in = q * jnp.exp(cum)
    scores = (jnp.where(diag, _nt(qt.astype(BF16), kt.astype(BF16)), 0.0)
              + jnp.where(cross, _nt(q_in.astype(BF16), k_st.astype(BF16)), 0.0))
    intra = _nn(scores.astype(BF16), v_bf)
    q_start = q_in * jnp.where(second, jnp.exp(tot_first), 1.0)
    inter = _nt(q_start.astype(BF16), st.astype(BF16))
    return st_new, intra, inter, jnp.abs(rel)


def _hgrn_intra_exact(z, v_bf, q, lb, masks, rev, cum_scr, k_scr):
    c = HG_CHUNK
    n = 2 * c
    tri, diag, cross = masks
    k, cum = _hgrn_gates(z, lb, tri)
    rows = lax.broadcasted_iota(jnp.int32, cum.shape, 0)
    if rev:
        cum = cum + jnp.where(rows < c, cum[c:c + 1, :], 0.0)
    else:
        cum = cum + jnp.where(rows >= c, cum[c - 1:c, :], 0.0)
    col = lax.broadcasted_iota(jnp.int32, (n, n), 1)
    cum_scr[...] = cum
    k_scr[...] = k

    def key_column(s, a):
        d = jnp.exp(jnp.minimum(cum - cum_scr[pl.ds(s, 1), :], 0.0))
        return jnp.where(col == s, jnp.sum(q * d * k_scr[pl.ds(s, 1), :], axis=-1, keepdims=True), a)

    a = lax.fori_loop(0, n, key_column, jnp.zeros((n, n), F32))
    return _nn(jnp.where(diag | cross, a, 0.0).astype(BF16), v_bf)


def _silu(x):
    return x * jax.nn.sigmoid(x)


def _hgrn_kernel(lb_ref, ng_ref, czf_ref, czb_ref, ci_ref, q_ref, zf_ref, zb_ref, i_ref, g_ref,
                 o_ref, of_scr, ob_scr, cum_scr, k_scr):
    n = 2 * HG_CHUNK
    n_ctx = czf_ref.shape[0] // n
    n_lat = q_ref.shape[0] // n
    un = math.gcd(HG_UNROLL, n_lat)
    masks_f = _hgrn_masks(False)
    masks_b = _hgrn_masks(True)
    lb_f = lb_ref[0:1, :]
    lb_b = lb_ref[1:2, :]

    def rows_of(block):
        return pl.ds(pl.multiple_of(block * n, n), n)

    def ctx_body(j, carry):
        sf, sb = carry
        rf = rows_of(j)
        rb = rows_of(n_ctx - 1 - j)
        sf = _hgrn_block(sf, czf_ref[rf, :].astype(F32), ci_ref[rf, :], None, lb_f, masks_f, False)[0]
        sb = _hgrn_block(sb, czb_ref[rb, :].astype(F32), ci_ref[rb, :], None, lb_b, masks_b, True)[0]
        return sf, sb

    s0 = jnp.zeros((HG_EXPAND, HG_EXPAND), F32)
    sf, sb = lax.fori_loop(0, n_ctx, ctx_body, (s0, s0))

    def lat_body(j, carry):
        sf, sb = carry
        worst = None
        redo = []
        for u in range(un):
            rf = rows_of(j * un + u)
            rb = rows_of(n_lat - 1 - (j * un + u))
            sf, intra_f, inter_f, ex_f = _hgrn_block(sf, zf_ref[rf, :].astype(F32), i_ref[rf, :],
                                                     _silu(q_ref[rf, :].astype(F32)), lb_f, masks_f, False)
            sb, intra_b, inter_b, ex_b = _hgrn_block(sb, zb_ref[rb, :].astype(F32), i_ref[rb, :],
                                                     _silu(q_ref[rb, :].astype(F32)), lb_b, masks_b, True)
            of_scr[rf, :] = intra_f + inter_f
            ob_scr[rb, :] = intra_b + inter_b
            ex = jnp.maximum(ex_f, ex_b)
            worst = ex if worst is None else jnp.maximum(worst, ex)
            redo.append((rf, rb, inter_f, inter_b))

        @pl.when(jnp.max(worst) > EXP_CLAMP)
        def _():
            for rf, rb, inter_f, inter_b in redo:
                of_scr[rf, :] = inter_f + _hgrn_intra_exact(
                    zf_ref[rf, :].astype(F32), i_ref[rf, :], _silu(q_ref[rf, :].astype(F32)), lb_f, masks_f, False,
                    cum_scr, k_scr)
                ob_scr[rb, :] = inter_b + _hgrn_intra_exact(
                    zb_ref[rb, :].astype(F32), i_ref[rb, :], _silu(q_ref[rb, :].astype(F32)), lb_b, masks_b, True,
                    cum_scr, k_scr)

        return sf, sb

    lax.fori_loop(0, n_lat // un, lat_body, (sf, sb))

    rows = 256
    ng = ng_ref[...]

    def out_body(j, _):
        r = pl.ds(pl.multiple_of(j * rows, rows), rows)
        o = of_scr[r, :] + ob_scr[r, :]
        o = o * lax.rsqrt(jnp.mean(o * o, axis=-1, keepdims=True) + RMS_EPS) * ng
        o_ref[r, :] = (o * _silu(g_ref[r, :].astype(F32))).astype(BF16)
        return 0

    lax.fori_loop(0, q_ref.shape[0] // rows, out_body, 0)


def _hgrn(cparts, parts, lb, norm_g, n_heads):
    b, t, _ = parts.shape
    tc = cparts.shape[1]
    w = HG_EXPAND

    def col(part):
        return lambda bi, hi: (bi, 0, part * n_heads + hi)

    cspec = lambda part: pl.BlockSpec((None, tc, w), col(part))
    lspec = lambda part: pl.BlockSpec((None, t, w), col(part))
    return pl.pallas_call(
        _hgrn_kernel,
        grid=(b, n_heads),
        in_specs=[pl.BlockSpec((2, w), lambda bi, hi: (0, hi)),
                  pl.BlockSpec((1, w), lambda bi, hi: (0, hi)),
                  cspec(1), cspec(2), cspec(3),
                  lspec(0), lspec(1), lspec(2), lspec(3), lspec(4)],
        out_specs=pl.BlockSpec((None, t, w), lambda bi, hi: (bi, 0, hi)),
        out_shape=jax.ShapeDtypeStruct((b, t, n_heads * w), BF16),
        scratch_shapes=[pltpu.VMEM((t, w), F32), pltpu.VMEM((t, w), F32),
                        pltpu.VMEM((2 * HG_CHUNK, w), F32), pltpu.VMEM((2 * HG_CHUNK, w), F32)],
        compiler_params=_cparams(("parallel", "parallel")),
        name="hgrn2",
    )(lb, norm_g.reshape(1, -1), cparts, cparts, cparts, parts, parts, parts, parts, parts)


def _rope(x, cos, sin):
    lane = lax.broadcasted_iota(jnp.int32, x.shape, 1)
    swapped = jnp.where((lane % 32) < 16,
                        pltpu.roll(x, LANES - 16, axis=1),
                        pltpu.roll(x, 16, axis=1))
    return x * cos + swapped * sin


def _da_kernel(lam_ref, q_ref, kc_ref, kl_ref, vc_ref, vl_ref, cq_ref, sq_ref, ck_ref, sk_ref, g_ref,
               o_ref, k_scr, v_scr):
    tc = kc_ref.shape[0]
    t = kl_ref.shape[0]
    dv = DA_V_DIM

    @pl.when(pl.program_id(2) == 0)
    def _():
        lane = lax.broadcasted_iota(jnp.int32, (tc + t, LANES), 1)
        v_scr[:, dv:dv + LANES] = jnp.where(lane == 0, 1.0, 0.0).astype(BF16)
        k_scr[:, 0:tc] = kc_ref[...].astype(F32).T.astype(BF16)
        v_scr[0:tc, 0:dv] = vc_ref[...]
        v_scr[tc:tc + t, 0:dv] = vl_ref[...]
        rows = min(512, t)
        for j in range(t // rows):
            r = slice(j * rows, (j + 1) * rows)
            kr = _rope(kl_ref[r, :].astype(F32), ck_ref[r, :], sk_ref[r, :])
            k_scr[:, tc + j * rows:tc + (j + 1) * rows] = kr.T.astype(BF16)

    lam = lam_ref[0]
    out_scale = lam_ref[1]
    qscale = (DA_HEAD_DIM ** -0.5) * math.log2(math.e)
    k = k_scr[...]
    v1 = v_scr[...]

    def branch(qm):
        s = _nn(qm.astype(BF16), k)
        p = jnp.exp2(s - jnp.max(s, axis=-1, keepdims=True)).astype(BF16)
        ov = _nn(p, v1)
        return ov[:, 0:dv] / ov[:, dv:dv + 1]

    sub = min(DA_Q_SUB, q_ref.shape[0])
    for j in range(q_ref.shape[0] // sub):
        r = slice(j * sub, (j + 1) * sub)
        qr = _rope(q_ref[r, :].astype(F32), cq_ref[r, :], sq_ref[r, :]) * qscale
        lane = lax.broadcasted_iota(jnp.int32, qr.shape, 1)
        o = (branch(jnp.where(lane < DA_HEAD_DIM, qr, 0.0))
             - lam * branch(jnp.where(lane >= DA_HEAD_DIM, qr, 0.0)))
        o = o * lax.rsqrt(jnp.mean(o * o, axis=-1, keepdims=True) + RMS_EPS)
        o_ref[r, :] = (o * g_ref[...] * out_scale).astype(BF16)


def _diff_attn(cparts, parts, lam2, subln_g, cos_t, sin_t, n_heads, col0, tq):
    b, t, _ = parts.shape
    tc = cparts.shape[1]
    w = LANES

    def col(part):
        return lambda bi, hi, qi: (bi, 0, col0 + part * n_heads + hi)

    return pl.pallas_call(
        _da_kernel,
        grid=(b, n_heads, t // tq),
        in_specs=[pl.BlockSpec(memory_space=pltpu.SMEM),
                  pl.BlockSpec((None, tq, w), lambda bi, hi, qi: (bi, qi, col0 + hi)),
                  pl.BlockSpec((None, tc, w), col(1)),
                  pl.BlockSpec((None, t, w), col(1)),
                  pl.BlockSpec((None, tc, w), col(2)),
                  pl.BlockSpec((None, t, w), col(2)),
                  pl.BlockSpec((tq, w), lambda bi, hi, qi: (qi, 0)),
                  pl.BlockSpec((tq, w), lambda bi, hi, qi: (qi, 0)),
                  pl.BlockSpec((t, w), lambda bi, hi, qi: (0, 0)),
                  pl.BlockSpec((t, w), lambda bi, hi, qi: (0, 0)),
                  pl.BlockSpec((1, w), lambda bi, hi, qi: (0, hi))],
        out_specs=pl.BlockSpec((None, tq, w), lambda bi, hi, qi: (bi, qi, hi)),
        out_shape=jax.ShapeDtypeStruct((b, t, n_heads * w), BF16),
        scratch_shapes=[pltpu.VMEM((w, tc + t), BF16), pltpu.VMEM((tc + t, DA_V_DIM + w), BF16)],
        compiler_params=_cparams(("parallel", "parallel", "arbitrary")),
        name="diff_attn",
    )(lam2, parts, cparts, parts, cparts, parts, cos_t, sin_t, cos_t, sin_t, subln_g.reshape(1, -1))


def _outproj_kernel(hg_ref, da_ref, x_ref, wa_ref, wb_ref, g1_ref, sc_ref, sh_ref, ng_ref, x1_ref, h2_ref):
    mix = _nn(hg_ref[...], wa_ref[...]) + _nn(da_ref[...], wb_ref[...])
    x1 = x_ref[...] + g1_ref[0] * mix
    x1_ref[...] = x1
    y = x1 * lax.rsqrt(jnp.mean(x1 * x1, axis=-1, keepdims=True) + RMS_EPS) * ng_ref[...]
    h2_ref[...] = (y * (1.0 + sc_ref[0]) + sh_ref[0]).astype(BF16)


def _outproj(hg, da, x2, w_out, g1, sc2, sh2, norm_g, rows_per_mod, tm):
    m, d = x2.shape
    half = hg.shape[1]
    bpm = rows_per_mod // tm
    mod = pl.BlockSpec((1, 1, d), lambda i: (i // bpm, 0, 0))
    return pl.pallas_call(
        _outproj_kernel,
        grid=(m // tm,),
        in_specs=[pl.BlockSpec((tm, half), lambda i: (i, 0)),
                  pl.BlockSpec((tm, half), lambda i: (i, 0)),
                  pl.BlockSpec((tm, d), lambda i: (i, 0)),
                  pl.BlockSpec((half, d), lambda i: (0, 0)),
                  pl.BlockSpec((half, d), lambda i: (1, 0)),
                  mod, mod, mod,
                  pl.BlockSpec((1, d), lambda i: (0, 0))],
        out_specs=[pl.BlockSpec((tm, d), lambda i: (i, 0)),
                   pl.BlockSpec((tm, d), lambda i: (i, 0))],
        out_shape=[jax.ShapeDtypeStruct((m, d), F32), jax.ShapeDtypeStruct((m, d), BF16)],
        compiler_params=_cparams(("parallel",)),
        name="outproj",
    )(hg, da, x2, w_out, w_out, g1, sc2, sh2, norm_g)


def _top_values(s, n):
    vals = []
    rank = jnp.full(s.shape, float(n), F32)
    for r in range(n):
        m = jnp.max(s, axis=0, keepdims=True)
        hit = s == m
        vals.append(m)
        rank = jnp.where(hit, float(r), rank)
        s = jnp.where(hit, NEG_INF, s)
    return vals, rank


def _route_kernel(h_ref, wq_ref, sk_ref, n1_ref, a_ref, r2_ref, b_ref):
    kk = PEER_TOPK
    q = _nn(h_ref[...], wq_ref[...]).astype(BF16)
    for h in range(PEER_HEADS):
        s1 = _nt(sk_ref[2 * h], q[:, (2 * h) * LANES:(2 * h + 1) * LANES])
        s2 = _nt(sk_ref[2 * h + 1], q[:, (2 * h + 1) * LANES:(2 * h + 2) * LANES])
        v1, rank1 = _top_values(s1, kk)
        v2, rank2 = _top_values(s2, kk)
        sv2 = jnp.concatenate(v2, axis=0)
        cands = [v1[0] + sv2]
        cands += [v1[r] + sv2[0:8, :] for r in range(1, 8)]
        cands.append(jnp.concatenate(v1[8:16], axis=0) + v2[0])
        top = v1[0] + v2[0]
        z = jnp.zeros_like(top)
        tau = top
        work = cands
        for _ in range(kk):
            m8 = work[1]
            for cnd in work[2:]:
                m8 = jnp.maximum(m8, cnd)
            tau = jnp.maximum(jnp.max(work[0], axis=0, keepdims=True), jnp.max(m8, axis=0, keepdims=True))
            z = z + jnp.exp(tau - top)
            work = [jnp.where(cnd == tau, NEG_INF, cnd) for cnd in work]
        cnt = [jnp.sum(jnp.where(cands[r] >= tau, 1.0, 0.0), axis=0, keepdims=True) for r in range(8)]
        tail = jnp.where(cands[8] >= tau, 1.0, 0.0)
        cnt += [tail[r:r + 1, :] for r in range(8)]
        n1 = jnp.zeros_like(s1)
        for r in range(kk):
            n1 = jnp.where(rank1 == float(r), cnt[r], n1)
        n1_ref[h] = n1
        a_ref[h] = jnp.exp(s1 - v1[0])
        r2_ref[h] = rank2.astype(BF16)
        b_ref[h] = (jnp.exp(s2 - v2[0]) / z).astype(BF16)


def _route(h2, wq, sk, tp):
    p, d = h2.shape
    nq = wq.shape[1]
    tab = lambda: pl.BlockSpec((PEER_HEADS, PEER_NKEYS, tp), lambda i: (0, 0, i))
    return pl.pallas_call(
        _route_kernel,
        grid=(p // tp,),
        in_specs=[pl.BlockSpec((tp, d), lambda i: (i, 0)),
                  pl.BlockSpec((d, nq), lambda i: (0, 0)),
                  pl.BlockSpec((2 * PEER_HEADS, PEER_NKEYS, LANES), lambda i: (0, 0, 0))],
        out_specs=[tab(), tab(), tab(), tab()],
        out_shape=[jax.ShapeDtypeStruct((PEER_HEADS, PEER_NKEYS, p), F32),
                   jax.ShapeDtypeStruct((PEER_HEADS, PEER_NKEYS, p), F32),
                   jax.ShapeDtypeStruct((PEER_HEADS, PEER_NKEYS, p), BF16),
                   jax.ShapeDtypeStruct((PEER_HEADS, PEER_NKEYS, p), BF16)],
        compiler_params=_cparams(("parallel",)),
        name="peer_route",
    )(h2, wq, sk)


PEER_SUB = 512
PEER_LANE_CHUNK = 256


def _peer_kernel(x1_ref, g2_ref, fg_ref, h_ref, n1_ref, a_ref, r2_ref, b_ref, u_ref, v_ref, o_ref,
                 acc, ht_scr, act_a, act_b):
    e = pl.program_id(1)
    te, tp = ht_scr.shape
    n_slab = te // PEER_NKEYS
    slabs_per_sub = PEER_SUB // PEER_NKEYS
    zero = jnp.zeros((), BF16)

    @pl.when(e == 0)
    def _():
        acc[...] = jnp.zeros_like(acc)

    n_sub = te // PEER_SUB
    act_bufs = (act_a, act_b)

    def first_matmul(sub):
        act_bufs[sub % 2][...] = _nt(u_ref[sub * PEER_SUB:(sub + 1) * PEER_SUB, :], h_ref[...])

    first_matmul(0)
    for sub in range(n_sub):
        if sub + 1 < n_sub:
            first_matmul(sub + 1)
        act = act_bufs[sub % 2]
        for s in range(slabs_per_sub):
            sl = sub * slabs_per_sub + s
            i1 = e * n_slab + sl
            n1rows = [n1_ref[h, pl.ds(i1, 1), :] for h in range(PEER_HEADS)]
            arows = [a_ref[h, pl.ds(i1, 1), :] for h in range(PEER_HEADS)]
            for lc in range(tp // PEER_LANE_CHUNK):
                lanes = slice(lc * PEER_LANE_CHUNK, (lc + 1) * PEER_LANE_CHUNK)
                tile = (BF16_SUBLANES, PEER_LANE_CHUNK)
                n1b = [jnp.broadcast_to(r[:, lanes], tile).astype(BF16) for r in n1rows]
                ab = [jnp.broadcast_to(r[:, lanes], tile).astype(BF16) for r in arows]
                for g in range(PEER_NKEYS // BF16_SUBLANES):
                    keys = slice(g * BF16_SUBLANES, (g + 1) * BF16_SUBLANES)
                    gate = None
                    for h in range(PEER_HEADS):
                        term = jnp.where(r2_ref[h, keys, lanes] < n1b[h], b_ref[h, keys, lanes], zero) * ab[h]
                        gate = term if gate is None else gate + term
                    rows = slice(s * PEER_NKEYS + g * BF16_SUBLANES, s * PEER_NKEYS + (g + 1) * BF16_SUBLANES)
                    a = act[rows, lanes].astype(BF16)
                    gelu = 0.5 * a * (1.0 + lax.erf(a * math.sqrt(0.5)))
                    rows = slice(sl * PEER_NKEYS + g * BF16_SUBLANES, sl * PEER_NKEYS + (g + 1) * BF16_SUBLANES)
                    ht_scr[rows, lanes] = gate * gelu
    acc[...] += _tn(ht_scr[...], v_ref[...])

    @pl.when(e == pl.num_programs(1) - 1)
    def _():
        x = x1_ref[...] + g2_ref[0] * acc[...]
        o_ref[...] = x * lax.rsqrt(jnp.mean(x * x, axis=-1, keepdims=True) + RMS_EPS) * fg_ref[...]


def _peer(x1, g2, fg, h2, n1, a, r2, bb, u_bf, v_bf, rows_per_mod, tp, te):
    p, d = x1.shape
    n_exp = u_bf.shape[0]
    bpm = rows_per_mod // tp
    tab = lambda: pl.BlockSpec((PEER_HEADS, PEER_NKEYS, tp), lambda i, e: (0, 0, i))
    return pl.pallas_call(
        _peer_kernel,
        grid=(p // tp, n_exp // te),
        in_specs=[pl.BlockSpec((tp, d), lambda i, e: (i, 0), pipeline_mode=pl.Buffered(1)),
                  pl.BlockSpec((1, 1, d), lambda i, e: (i // bpm, 0, 0)),
                  pl.BlockSpec((1, d), lambda i, e: (0, 0)),
                  pl.BlockSpec((tp, d), lambda i, e: (i, 0), pipeline_mode=pl.Buffered(1)),
                  tab(), tab(), tab(), tab(),
                  pl.BlockSpec((te, d), lambda i, e: (e, 0)),
                  pl.BlockSpec((te, d), lambda i, e: (e, 0))],
        out_specs=pl.BlockSpec((tp, d), lambda i, e: (i, 0)),
        out_shape=jax.ShapeDtypeStruct((p, d), F32),
        scratch_shapes=[pltpu.VMEM((tp, d), F32),
                        pltpu.VMEM((te, tp), BF16),
                        pltpu.VMEM((PEER_SUB, tp), F32),
                        pltpu.VMEM((PEER_SUB, tp), F32)],
        compiler_params=_cparams(("parallel", "arbitrary")),
        name="peer_experts",
    )(x1, g2, fg, h2, n1, a, r2, bb, u_bf, v_bf)


def _rope_tables(t):
    rows = t // GRID_W
    row = jnp.repeat(jnp.arange(rows, dtype=F32), GRID_W)
    colp = jnp.tile(jnp.arange(GRID_W, dtype=F32), rows)
    n_freq = DA_HEAD_DIM // 4
    inv_freq = ROPE_BASE ** (-jnp.arange(n_freq, dtype=F32) / n_freq)
    ar = row[:, None] * inv_freq[None, :]
    ac = colp[:, None] * inv_freq[None, :]
    cos64 = jnp.concatenate([jnp.cos(ar), jnp.cos(ar), jnp.cos(ac), jnp.cos(ac)], axis=-1)
    sin64 = jnp.concatenate([-jnp.sin(ar), jnp.sin(ar), -jnp.sin(ac), jnp.sin(ac)], axis=-1)
    return jnp.tile(cos64, (1, 2)), jnp.tile(sin64, (1, 2))


def _layer(x, c, ctx, c_ctx, w_ada, b_ada, norm1_g, w_in, lb, hg_norm_g, lam, lam_init, da_subln_g,
           w_out, norm2_g, peer_w_query, peer_sub_keys, peer_u, peer_v, final_norm_g):
    b, t, d = x.shape
    tc = ctx.shape[1]
    hg_width = hg_norm_g.shape[0]
    hg_heads = hg_width // HG_EXPAND
    da_heads = da_subln_g.shape[0] // DA_V_DIM

    cond = jnp.zeros((8, d), F32).at[0:b].set(c).at[b].set(c_ctx)
    mod = _adaln(cond, w_ada, b_ada)
    sh1, sc1, g1, sh2, sc2, g2 = [m[0:b].reshape(b, 1, d) for m in jnp.split(mod, 6, axis=-1)]
    csh1, csc1 = [m[b:b + 1].reshape(1, 1, d) for m in jnp.split(mod, 6, axis=-1)[0:2]]

    w_in_bf = w_in.astype(BF16)
    n_cols = w_in.shape[1]
    g1n = norm1_g.reshape(1, d)
    tm_lat = min(1024, t)
    parts = _inproj(x.reshape(b * t, d), sc1, sh1, g1n, w_in_bf, t, tm_lat, 1024).reshape(b, t, n_cols)
    tm_ctx = min(512, b * tc)
    cparts = _inproj(ctx.reshape(b * tc, d), csc1, csh1, g1n, w_in_bf, b * tc, tm_ctx, 1024).reshape(b, tc, n_cols)

    hg = _hgrn(cparts, parts, lb, hg_norm_g, hg_heads)
    cos_t, sin_t = _rope_tables(t)
    lam2 = jnp.stack([lam, jnp.asarray(1.0 - lam_init, F32)]).astype(F32)
    da = _diff_attn(cparts, parts, lam2, da_subln_g, cos_t, sin_t, da_heads, 5 * hg_heads, min(2048, t))

    x2 = x.reshape(b * t, d)
    x1, h2 = _outproj(hg.reshape(b * t, -1), da.reshape(b * t, -1), x2, w_out.astype(BF16),
                      g1, sc2, sh2, norm2_g.reshape(1, d), t, min(512, t))

    sk = peer_sub_keys.reshape(2 * PEER_HEADS, PEER_NKEYS, -1).astype(BF16)
    n1, a, r2, bb = _route(h2, peer_w_query.astype(BF16), sk, min(256, t))
    out = _peer(x1, g2, final_norm_g.reshape(1, d), h2, n1, a, r2, bb,
                peer_u.astype(BF16), peer_v.astype(BF16), t, min(512, t), 1024)
    return out.reshape(b, t, d)


def kernel(x, c, ctx, c_ctx, w_ada, b_ada, norm1_g, w_in, hg_gamma, hg_norm_g, da_lambda_q1, da_lambda_k1,
           da_lambda_q2, da_lambda_k2, da_subln_g, w_out, norm2_g, peer_w_query, peer_sub_keys, peer_u,
           peer_v, final_norm_g):
    depth = w_ada.shape[0]
    assert depth == 1, "single-layer block: the context stream is never updated"
    lb_all = jnp.cumsum(jax.nn.softmax(hg_gamma.astype(F32), axis=0), axis=0)
    l = 0
    lam_init = 0.8 - 0.6 * math.exp(-0.3 * l)
    lam = (jnp.exp(jnp.sum(da_lambda_q1[l].astype(F32) * da_lambda_k1[l].astype(F32)))
           - jnp.exp(jnp.sum(da_lambda_q2[l].astype(F32) * da_lambda_k2[l].astype(F32))) + lam_init)
    return _layer(x, c, ctx, c_ctx, w_ada[l], b_ada[l], norm1_g[l], w_in[l], lb_all[l], hg_norm_g[l],
                  lam, lam_init, da_subln_g[l], w_out[l], norm2_g[l], peer_w_query[l], peer_sub_keys[l],
                  peer_u[l], peer_v[l], final_norm_g)
```

```python
import math
from typing import NamedTuple

import jax
import jax.numpy as jnp
from jax import lax
from jax.experimental import pallas as pl
from jax.experimental.pallas import tpu as pltpu

F32 = jnp.float32
BF16 = jnp.bfloat16

GRID_W = 64
HG_EXPAND = 128
HG_CHUNK = 64
HG_UNROLL = 4
DA_HEAD_DIM = 64
DA_V_DIM = 128
DA_Q_SUB = 256
ROPE_BASE = 10000.0
PEER_HEADS = 8
PEER_NKEYS = 128
PEER_TOPK = 16
RMS_EPS = 1e-6
LANES = 128
BF16_SUBLANES = 16
EXP_CLAMP = 80.0
V7X_VMEM_BYTES = 64 * 1024 * 1024
VMEM_LIMIT = V7X_VMEM_BYTES - 8 * 1024 * 1024
NEG_INF = float("-inf")


def _nt(a, b):
    return lax.dot_general(a, b, (((1,), (1,)), ((), ())), preferred_element_type=F32)


def _tn(a, b):
    return lax.dot_general(a, b, (((0,), (0,)), ((), ())), preferred_element_type=F32)


def _nn(a, b):
    return jnp.dot(a, b, preferred_element_type=F32)


def _cparams(sem):
    return pltpu.CompilerParams(dimension_semantics=sem, vmem_limit_bytes=VMEM_LIMIT)


def _adaln_kernel(c_ref, w_ref, b_ref, o_ref):
    c = c_ref[...]
    a = (c * jax.nn.sigmoid(c)).astype(BF16)
    o_ref[...] = _nn(a, w_ref[...].astype(BF16)) + b_ref[...]


def _adaln(cond, w, b, tn):
    d, n = w.shape
    return pl.pallas_call(
        _adaln_kernel,
        grid=(n // tn,),
        in_specs=[pl.BlockSpec((8, d), lambda j: (0, 0)),
                  pl.BlockSpec((d, tn), lambda j: (0, j)),
                  pl.BlockSpec((1, tn), lambda j: (0, j))],
        out_specs=pl.BlockSpec((8, tn), lambda j: (0, j)),
        out_shape=jax.ShapeDtypeStruct((8, n), F32),
        compiler_params=_cparams(("arbitrary",)),
        name="adaln",
    )(cond, w, b.reshape(1, n))


def _inproj_kernel(x_ref, sc_ref, sh_ref, g_ref, w_ref, o_ref, h_scr):
    @pl.when(pl.program_id(1) == 0)
    def _():
        x = x_ref[...]
        y = x * lax.rsqrt(jnp.mean(x * x, axis=-1, keepdims=True) + RMS_EPS) * g_ref[...]
        h_scr[...] = (y * (1.0 + sc_ref[0]) + sh_ref[0]).astype(BF16)

    o_ref[...] = _nn(h_scr[...], w_ref[...]).astype(BF16)


def _inproj(x2, sc, sh, g, w, rows_per_mod, tm, tn):
    m, d = x2.shape
    n = w.shape[1]
    bpm = rows_per_mod // tm
    return pl.pallas_call(
        _inproj_kernel,
        grid=(m // tm, n // tn),
        in_specs=[pl.BlockSpec((tm, d), lambda i, j: (i, 0)),
                  pl.BlockSpec((1, 1, d), lambda i, j: (i // bpm, 0, 0)),
                  pl.BlockSpec((1, 1, d), lambda i, j: (i // bpm, 0, 0)),
                  pl.BlockSpec((1, d), lambda i, j: (0, 0)),
                  pl.BlockSpec((d, tn), lambda i, j: (0, j))],
        out_specs=pl.BlockSpec((tm, tn), lambda i, j: (i, j)),
        out_shape=jax.ShapeDtypeStruct((m, n), BF16),
        scratch_shapes=[pltpu.VMEM((tm, d), BF16)],
        compiler_params=_cparams(("parallel", "arbitrary")),
        name="inproj",
    )(x2, sc, sh, g, w)


def _hgrn_gates(z, lb, tri):
    f = lb + (1.0 - lb) * jax.nn.sigmoid(z)
    lf = jnp.log(f)
    hi = lf.astype(BF16)
    lo = (lf - hi.astype(F32)).astype(BF16)
    both = _nn(tri, jnp.concatenate([hi, lo], axis=1))
    w = lf.shape[1]
    return 1.0 - f, both[:, 0:w] + both[:, w:2 * w]


def _hgrn_masks(rev):
    n = 2 * HG_CHUNK
    row = lax.broadcasted_iota(jnp.int32, (n, n), 0)
    col = lax.broadcasted_iota(jnp.int32, (n, n), 1)
    same = (row >= HG_CHUNK) == (col >= HG_CHUNK)
    before = (col >= row) if rev else (row >= col)
    diag = same & before
    cross = (~same) & before
    return jnp.where(diag, 1.0, 0.0).astype(BF16), diag, cross


def _hgrn_block(st, z, v_bf, q, lb, masks, rev):
    c = HG_CHUNK
    tri, diag, cross = masks
    k, cum = _hgrn_gates(z, lb, tri)
    second = lax.broadcasted_iota(jnp.int32, cum.shape, 0) < c if rev else \
        lax.broadcasted_iota(jnp.int32, cum.shape, 0) >= c
    tot_lo = cum[0:1, :] if rev else cum[c - 1:c, :]
    tot_hi = cum[c:c + 1, :] if rev else cum[2 * c - 1:2 * c, :]
    tot_first, tot_second = (tot_hi, tot_lo) if rev else (tot_lo, tot_hi)
    tot_own = jnp.where(lax.broadcasted_iota(jnp.int32, cum.shape, 0) < c, tot_lo, tot_hi)
    k_st = k * jnp.exp(tot_own - cum)
    k_end = k_st * jnp.where(second, 1.0, jnp.exp(tot_second))
    st_new = st * jnp.exp(tot_first + tot_second) + _tn(v_bf, k_end.astype(BF16))
    if q is None:
        return st_new, None, None, None
    mid_lo = cum[c // 2:c // 2 + 1, :] if rev else cum[c // 2 - 1:c // 2, :]
    mid_hi = cum[c + c // 2:c + c // 2 + 1, :] if rev else cum[c + c // 2 - 1:c + c // 2, :]
    rel = cum - jnp.where(lax.broadcasted_iota(jnp.int32, cum.shape, 0) < c, mid_lo, mid_hi)
    qt = q * jnp.exp(jnp.minimum(rel, EXP_CLAMP))
    kt = k * jnp.exp(jnp.minimum(-rel, EXP_CLAMP))
    q_in = q * jnp.exp(cum)
    scores = (jnp.where(diag, _nt(qt.astype(BF16), kt.astype(BF16)), 0.0)
              + jnp.where(cross, _nt(q_in.astype(BF16), k_st.astype(BF16)), 0.0))
    intra = _nn(scores.astype(BF16), v_bf)
    q_start = q_in * jnp.where(second, jnp.exp(tot_first), 1.0)
    inter = _nt(q_start.astype(BF16), st.astype(BF16))
    return st_new, intra, inter, jnp.abs(rel)


def _hgrn_intra_exact(z, v_bf, q, lb, masks, rev, cum_scr, k_scr):
    c = HG_CHUNK
    n = 2 * c
    tri, diag, cross = masks
    k, cum = _hgrn_gates(z, lb, tri)
    rows = lax.broadcasted_iota(jnp.int32, cum.shape, 0)
    if rev:
        cum = cum + jnp.where(rows < c, cum[c:c + 1, :], 0.0)
    else:
        cum = cum + jnp.where(rows >= c, cum[c - 1:c, :], 0.0)
    col = lax.broadcasted_iota(jnp.int32, (n, n), 1)
    cum_scr[...] = cum
    k_scr[...] = k

    def key_column(s, a):
        d = jnp.exp(jnp.minimum(cum - cum_scr[pl.ds(s, 1), :], 0.0))
        return jnp.where(col == s, jnp.sum(q * d * k_scr[pl.ds(s, 1), :], axis=-1, keepdims=True), a)

    a = lax.fori_loop(0, n, key_column, jnp.zeros((n, n), F32))
    return _nn(jnp.where(diag | cross, a, 0.0).astype(BF16), v_bf)


def _silu(x):
    return x * jax.nn.sigmoid(x)


def _hgrn_kernel(lb_ref, ng_ref, czf_ref, czb_ref, ci_ref, q_ref, zf_ref, zb_ref, i_ref, g_ref,
                 o_ref, of_scr, ob_scr, cum_scr, k_scr):
    n = 2 * HG_CHUNK
    n_ctx = czf_ref.shape[0] // n
    n_lat = q_ref.shape[0] // n
    un = math.gcd(HG_UNROLL, n_lat)
    masks_f = _hgrn_masks(False)
    masks_b = _hgrn_masks(True)
    lb_f = lb_ref[0:1, :]
    lb_b = lb_ref[1:2, :]

    def rows_of(block):
        return pl.ds(pl.multiple_of(block * n, n), n)

    def ctx_body(j, carry):
        sf, sb = carry
        rf = rows_of(j)
        rb = rows_of(n_ctx - 1 - j)
        sf = _hgrn_block(sf, czf_ref[rf, :].astype(F32), ci_ref[rf, :], None, lb_f, masks_f, False)[0]
        sb = _hgrn_block(sb, czb_ref[rb, :].astype(F32), ci_ref[rb, :], None, lb_b, masks_b, True)[0]
        return sf, sb

    s0 = jnp.zeros((HG_EXPAND, HG_EXPAND), F32)
    sf, sb = lax.fori_loop(0, n_ctx, ctx_body, (s0, s0))

    def lat_body(j, carry):
        sf, sb = carry
        worst = None
        redo = []
        for u in range(un):
            rf = rows_of(j * un + u)
            rb = rows_of(n_lat - 1 - (j * un + u))
            sf, intra_f, inter_f, ex_f = _hgrn_block(sf, zf_ref[rf, :].astype(F32), i_ref[rf, :],
                                                     _silu(q_ref[rf, :].astype(F32)), lb_f, masks_f, False)
            sb, intra_b, inter_b, ex_b = _hgrn_block(sb, zb_ref[rb, :].astype(F32), i_ref[rb, :],
                                                     _silu(q_ref[rb, :].astype(F32)), lb_b, masks_b, True)
            of_scr[rf, :] = intra_f + inter_f
            ob_scr[rb, :] = intra_b + inter_b
            ex = jnp.maximum(ex_f, ex_b)
            worst = ex if worst is None else jnp.maximum(worst, ex)
            redo.append((rf, rb, inter_f, inter_b))

        @pl.when(jnp.max(worst) > EXP_CLAMP)
        def _():
            for rf, rb, inter_f, inter_b in redo:
                of_scr[rf, :] = inter_f + _hgrn_intra_exact(
                    zf_ref[rf, :].astype(F32), i_ref[rf, :], _silu(q_ref[rf, :].astype(F32)), lb_f, masks_f, False,
                    cum_scr, k_scr)
                ob_scr[rb, :] = inter_b + _hgrn_intra_exact(
                    zb_ref[rb, :].astype(F32), i_ref[rb, :], _silu(q_ref[rb, :].astype(F32)), lb_b, masks_b, True,
                    cum_scr, k_scr)

        return sf, sb

    lax.fori_loop(0, n_lat // un, lat_body, (sf, sb))

    rows = 256
    ng = ng_ref[...]

    def out_body(j, _):
        r = pl.ds(pl.multiple_of(j * rows, rows), rows)
        o = of_scr[r, :] + ob_scr[r, :]
        o = o * lax.rsqrt(jnp.mean(o * o, axis=-1, keepdims=True) + RMS_EPS) * ng
        o_ref[r, :] = (o * _silu(g_ref[r, :].astype(F32))).astype(BF16)
        return 0

    lax.fori_loop(0, q_ref.shape[0] // rows, out_body, 0)


def _hgrn(cparts, parts, lb, norm_g, n_heads):
    b, t, _ = parts.shape
    tc = cparts.shape[1]
    w = HG_EXPAND

    def col(part):
        return lambda bi, hi: (bi, 0, part * n_heads + hi)

    cspec = lambda part: pl.BlockSpec((None, tc, w), col(part))
    lspec = lambda part: pl.BlockSpec((None, t, w), col(part))
    return pl.pallas_call(
        _hgrn_kernel,
        grid=(b, n_heads),
        in_specs=[pl.BlockSpec((2, w), lambda bi, hi: (0, hi)),
                  pl.BlockSpec((1, w), lambda bi, hi: (0, hi)),
                  cspec(1), cspec(2), cspec(3),
                  lspec(0), lspec(1), lspec(2), lspec(3), lspec(4)],
        out_specs=pl.BlockSpec((None, t, w), lambda bi, hi: (bi, 0, hi)),
        out_shape=jax.ShapeDtypeStruct((b, t, n_heads * w), BF16),
        scratch_shapes=[pltpu.VMEM((t, w), F32), pltpu.VMEM((t, w), F32),
                        pltpu.VMEM((2 * HG_CHUNK, w), F32), pltpu.VMEM((2 * HG_CHUNK, w), F32)],
        compiler_params=_cparams(("parallel", "parallel")),
        name="hgrn2",
    )(lb, norm_g.reshape(1, -1), cparts, cparts, cparts, parts, parts, parts, parts, parts)


def _rope(x, cos, sin):
    lane = lax.broadcasted_iota(jnp.int32, x.shape, 1)
    swapped = jnp.where((lane % 32) < 16,
                        pltpu.roll(x, LANES - 16, axis=1),
                        pltpu.roll(x, 16, axis=1))
    return x * cos + swapped * sin


def _da_kernel(lam_ref, q_ref, kc_ref, kl_ref, vc_ref, vl_ref, cq_ref, sq_ref, ck_ref, sk_ref, g_ref,
               o_ref, k_scr, v_scr):
    tc = kc_ref.shape[0]
    t = kl_ref.shape[0]
    dv = DA_V_DIM

    @pl.when(pl.program_id(2) == 0)
    def _():
        lane = lax.broadcasted_iota(jnp.int32, (tc + t, LANES), 1)
        v_scr[:, dv:dv + LANES] = jnp.where(lane == 0, 1.0, 0.0).astype(BF16)
        k_scr[:, 0:tc] = kc_ref[...].astype(F32).T.astype(BF16)
        v_scr[0:tc, 0:dv] = vc_ref[...]
        v_scr[tc:tc + t, 0:dv] = vl_ref[...]
        rows = min(512, t)
        for j in range(t // rows):
            r = slice(j * rows, (j + 1) * rows)
            kr = _rope(kl_ref[r, :].astype(F32), ck_ref[r, :], sk_ref[r, :])
            k_scr[:, tc + j * rows:tc + (j + 1) * rows] = kr.T.astype(BF16)

    lam = lam_ref[0]
    out_scale = lam_ref[1]
    qscale = (DA_HEAD_DIM ** -0.5) * math.log2(math.e)
    k = k_scr[...]
    v1 = v_scr[...]

    def branch(qm):
        s = _nn(qm.astype(BF16), k)
        p = jnp.exp2(s - jnp.max(s, axis=-1, keepdims=True)).astype(BF16)
        ov = _nn(p, v1)
        return ov[:, 0:dv] / ov[:, dv:dv + 1]

    sub = min(DA_Q_SUB, q_ref.shape[0])
    for j in range(q_ref.shape[0] // sub):
        r = slice(j * sub, (j + 1) * sub)
        qr = _rope(q_ref[r, :].astype(F32), cq_ref[r, :], sq_ref[r, :]) * qscale
        lane = lax.broadcasted_iota(jnp.int32, qr.shape, 1)
        o = (branch(jnp.where(lane < DA_HEAD_DIM, qr, 0.0))
             - lam * branch(jnp.where(lane >= DA_HEAD_DIM, qr, 0.0)))
        o = o * lax.rsqrt(jnp.mean(o * o, axis=-1, keepdims=True) + RMS_EPS)
        o_ref[r, :] = (o * g_ref[...] * out_scale).astype(BF16)


def _diff_attn(cparts, parts, lam2, subln_g, cos_t, sin_t, n_heads, col0, tq):
    b, t, _ = parts.shape
    tc = cparts.shape[1]
    w = LANES

    def col(part):
        return lambda bi, hi, qi: (bi, 0, col0 + part * n_heads + hi)

    return pl.pallas_call(
        _da_kernel,
        grid=(b, n_heads, t // tq),
        in_specs=[pl.BlockSpec(memory_space=pltpu.SMEM),
                  pl.BlockSpec((None, tq, w), lambda bi, hi, qi: (bi, qi, col0 + hi)),
                  pl.BlockSpec((None, tc, w), col(1)),
                  pl.BlockSpec((None, t, w), col(1)),
                  pl.BlockSpec((None, tc, w), col(2)),
                  pl.BlockSpec((None, t, w), col(2)),
                  pl.BlockSpec((tq, w), lambda bi, hi, qi: (qi, 0)),
                  pl.BlockSpec((tq, w), lambda bi, hi, qi: (qi, 0)),
                  pl.BlockSpec((t, w), lambda bi, hi, qi: (0, 0)),
                  pl.BlockSpec((t, w), lambda bi, hi, qi: (0, 0)),
                  pl.BlockSpec((1, w), lambda bi, hi, qi: (0, hi))],
        out_specs=pl.BlockSpec((None, tq, w), lambda bi, hi, qi: (bi, qi, hi)),
        out_shape=jax.ShapeDtypeStruct((b, t, n_heads * w), BF16),
        scratch_shapes=[pltpu.VMEM((w, tc + t), BF16), pltpu.VMEM((tc + t, DA_V_DIM + w), BF16)],
        compiler_params=_cparams(("parallel", "parallel", "arbitrary")),
        name="diff_attn",
    )(lam2, parts, cparts, parts, cparts, parts, cos_t, sin_t, cos_t, sin_t, subln_g.reshape(1, -1))


def _outproj_kernel(hg_ref, da_ref, x_ref, wa_ref, wb_ref, g1_ref, sc_ref, sh_ref, ng_ref, x1_ref, h2_ref):
    mix = _nn(hg_ref[...], wa_ref[...]) + _nn(da_ref[...], wb_ref[...])
    x1 = x_ref[...] + g1_ref[0] * mix
    x1_ref[...] = x1
    y = x1 * lax.rsqrt(jnp.mean(x1 * x1, axis=-1, keepdims=True) + RMS_EPS) * ng_ref[...]
    h2_ref[...] = (y * (1.0 + sc_ref[0]) + sh_ref[0]).astype(BF16)


def _outproj(hg, da, x2, w_out, g1, sc2, sh2, norm_g, rows_per_mod, tm):
    m, d = x2.shape
    half = hg.shape[1]
    bpm = rows_per_mod // tm
    mod = pl.BlockSpec((1, 1, d), lambda i: (i // bpm, 0, 0))
    return pl.pallas_call(
        _outproj_kernel,
        grid=(m // tm,),
        in_specs=[pl.BlockSpec((tm, half), lambda i: (i, 0)),
                  pl.BlockSpec((tm, half), lambda i: (i, 0)),
                  pl.BlockSpec((tm, d), lambda i: (i, 0)),
                  pl.BlockSpec((half, d), lambda i: (0, 0)),
                  pl.BlockSpec((half, d), lambda i: (1, 0)),
                  mod, mod, mod,
                  pl.BlockSpec((1, d), lambda i: (0, 0))],
        out_specs=[pl.BlockSpec((tm, d), lambda i: (i, 0)),
                   pl.BlockSpec((tm, d), lambda i: (i, 0))],
        out_shape=[jax.ShapeDtypeStruct((m, d), F32), jax.ShapeDtypeStruct((m, d), BF16)],
        compiler_params=_cparams(("parallel",)),
        name="outproj",
    )(hg, da, x2, w_out, w_out, g1, sc2, sh2, norm_g)


def _top_values(s, n):
    vals = []
    rank = jnp.full(s.shape, float(n), F32)
    for r in range(n):
        m = jnp.max(s, axis=0, keepdims=True)
        hit = s == m
        vals.append(m)
        rank = jnp.where(hit, float(r), rank)
        s = jnp.where(hit, NEG_INF, s)
    return vals, rank


def _route_tables(s1, s2):
    kk = PEER_TOPK
    v1, rank1 = _top_values(s1, kk)
    v2, rank2 = _top_values(s2, kk)
    sv2 = jnp.concatenate(v2, axis=0)
    cands = [v1[0] + sv2]
    cands += [v1[r] + sv2[0:8, :] for r in range(1, 8)]
    cands.append(jnp.concatenate(v1[8:16], axis=0) + v2[0])
    top = v1[0] + v2[0]
    z = jnp.zeros_like(top)
    tau = top
    work = cands
    for _ in range(kk):
        m8 = work[1]
        for cnd in work[2:]:
            m8 = jnp.maximum(m8, cnd)
        tau = jnp.maximum(jnp.max(work[0], axis=0, keepdims=True), jnp.max(m8, axis=0, keepdims=True))
        z = z + jnp.exp(tau - top)
        work = [jnp.where(cnd == tau, NEG_INF, cnd) for cnd in work]
    cnt = [jnp.sum(jnp.where(cands[r] >= tau, 1.0, 0.0), axis=0, keepdims=True) for r in range(8)]
    tail = jnp.where(cands[8] >= tau, 1.0, 0.0)
    cnt += [tail[r:r + 1, :] for r in range(8)]
    n1 = jnp.zeros_like(s1)
    for r in range(kk):
        n1 = jnp.where(rank1 == float(r), cnt[r], n1)
    return n1, jnp.exp(s1 - v1[0]), rank2.astype(BF16), (jnp.exp(s2 - v2[0]) / z).astype(BF16)


def _route_kernel(h_ref, wq_ref, sk_ref, n1_ref, a_ref, r2_ref, b_ref):
    tp = h_ref.shape[0]
    q = _nn(h_ref[...], wq_ref[...]).astype(BF16)
    for h in range(PEER_HEADS):
        s1 = _nt(sk_ref[2 * h], q[:, (2 * h) * LANES:(2 * h + 1) * LANES])
        s2 = _nt(sk_ref[2 * h + 1], q[:, (2 * h + 1) * LANES:(2 * h + 2) * LANES])
        for c in range(tp // LANES):
            lanes = slice(c * LANES, (c + 1) * LANES)
            n1, a, r2, b = _route_tables(s1[:, lanes], s2[:, lanes])
            n1_ref[h, :, lanes] = n1
            a_ref[h, :, lanes] = a
            r2_ref[h, :, lanes] = r2
            b_ref[h, :, lanes] = b


def _route(h2, wq, sk, tp):
    p, d = h2.shape
    nq = wq.shape[1]
    tab = lambda: pl.BlockSpec((PEER_HEADS, PEER_NKEYS, tp), lambda i: (0, 0, i))
    return pl.pallas_call(
        _route_kernel,
        grid=(p // tp,),
        in_specs=[pl.BlockSpec((tp, d), lambda i: (i, 0)),
                  pl.BlockSpec((d, nq), lambda i: (0, 0)),
                  pl.BlockSpec((2 * PEER_HEADS, PEER_NKEYS, LANES), lambda i: (0, 0, 0))],
        out_specs=[tab(), tab(), tab(), tab()],
        out_shape=[jax.ShapeDtypeStruct((PEER_HEADS, PEER_NKEYS, p), F32),
                   jax.ShapeDtypeStruct((PEER_HEADS, PEER_NKEYS, p), F32),
                   jax.ShapeDtypeStruct((PEER_HEADS, PEER_NKEYS, p), BF16),
                   jax.ShapeDtypeStruct((PEER_HEADS, PEER_NKEYS, p), BF16)],
        compiler_params=_cparams(("parallel",)),
        name="peer_route",
    )(h2, wq, sk)


PEER_SUB = 256
PEER_LANE_CHUNK = 256


def _peer_kernel(x1_ref, g2_ref, fg_ref, h_ref, n1_ref, a_ref, r2_ref, b_ref, u_ref, v_ref, o_ref,
                 acc, ht_scr, act_a, act_b):
    e = pl.program_id(1)
    te, tp = ht_scr.shape
    n_slab = te // PEER_NKEYS
    slabs_per_sub = PEER_SUB // PEER_NKEYS
    zero = jnp.zeros((), BF16)

    @pl.when(e == 0)
    def _():
        acc[...] = jnp.zeros_like(acc)

    n_sub = te // PEER_SUB
    act_bufs = (act_a, act_b)

    def first_matmul(sub):
        act_bufs[sub % 2][...] = _nt(u_ref[sub * PEER_SUB:(sub + 1) * PEER_SUB, :], h_ref[...])

    first_matmul(0)
    for sub in range(n_sub):
        if sub + 1 < n_sub:
            first_matmul(sub + 1)
        act = act_bufs[sub % 2]
        for s in range(slabs_per_sub):
            sl = sub * slabs_per_sub + s
            i1 = e * n_slab + sl
            n1rows = [n1_ref[h, pl.ds(i1, 1), :] for h in range(PEER_HEADS)]
            arows = [a_ref[h, pl.ds(i1, 1), :] for h in range(PEER_HEADS)]
            for lc in range(tp // PEER_LANE_CHUNK):
                lanes = slice(lc * PEER_LANE_CHUNK, (lc + 1) * PEER_LANE_CHUNK)
                tile = (BF16_SUBLANES, PEER_LANE_CHUNK)
                n_tiles = PEER_NKEYS // BF16_SUBLANES
                gates = [None] * n_tiles
                for h in range(PEER_HEADS):
                    n1b = jnp.broadcast_to(n1rows[h][:, lanes], tile).astype(BF16)
                    ab = jnp.broadcast_to(arows[h][:, lanes], tile).astype(BF16)
                    for g in range(n_tiles):
                        keys = slice(g * BF16_SUBLANES, (g + 1) * BF16_SUBLANES)
                        term = jnp.where(r2_ref[h, keys, lanes] < n1b, b_ref[h, keys, lanes], zero) * ab
                        gates[g] = term if gates[g] is None else gates[g] + term
                for g in range(n_tiles):
                    rows = slice(s * PEER_NKEYS + g * BF16_SUBLANES, s * PEER_NKEYS + (g + 1) * BF16_SUBLANES)
                    a = act[rows, lanes].astype(BF16)
                    gelu = 0.5 * a * (1.0 + lax.erf(a * math.sqrt(0.5)))
                    rows = slice(sl * PEER_NKEYS + g * BF16_SUBLANES, sl * PEER_NKEYS + (g + 1) * BF16_SUBLANES)
                    ht_scr[rows, lanes] = gates[g] * gelu
    acc[...] += _tn(ht_scr[...], v_ref[...])

    @pl.when(e == pl.num_programs(1) - 1)
    def _():
        x = x1_ref[...] + g2_ref[0] * acc[...]
        o_ref[...] = x * lax.rsqrt(jnp.mean(x * x, axis=-1, keepdims=True) + RMS_EPS) * fg_ref[...]


def _peer(x1, g2, fg, h2, n1, a, r2, bb, u_bf, v_bf, rows_per_mod, tp, te):
    p, d = x1.shape
    n_exp = u_bf.shape[0]
    bpm = rows_per_mod // tp
    tab = lambda: pl.BlockSpec((PEER_HEADS, PEER_NKEYS, tp), lambda i, e: (0, 0, i))
    return pl.pallas_call(
        _peer_kernel,
        grid=(p // tp, n_exp // te),
        in_specs=[pl.BlockSpec((tp, d), lambda i, e: (i, 0), pipeline_mode=pl.Buffered(1)),
                  pl.BlockSpec((1, 1, d), lambda i, e: (i // bpm, 0, 0)),
                  pl.BlockSpec((1, d), lambda i, e: (0, 0)),
                  pl.BlockSpec((tp, d), lambda i, e: (i, 0), pipeline_mode=pl.Buffered(1)),
                  tab(), tab(), tab(), tab(),
                  pl.BlockSpec((te, d), lambda i, e: (e, 0)),
                  pl.BlockSpec((te, d), lambda i, e: (e, 0))],
        out_specs=pl.BlockSpec((tp, d), lambda i, e: (i, 0)),
        out_shape=jax.ShapeDtypeStruct((p, d), F32),
        scratch_shapes=[pltpu.VMEM((tp, d), F32),
                        pltpu.VMEM((te, tp), BF16),
                        pltpu.VMEM((PEER_SUB, tp), F32),
                        pltpu.VMEM((PEER_SUB, tp), F32)],
        compiler_params=_cparams(("parallel", "arbitrary")),
        name="peer_experts",
    )(x1, g2, fg, h2, n1, a, r2, bb, u_bf, v_bf)


class _Tiles(NamedTuple):
    adaln_cols: int
    inproj_rows: int
    inproj_ctx_rows: int
    inproj_cols: int
    attn_queries: int
    outproj_rows: int
    route_tokens: int
    peer_tokens: int
    peer_experts: int


def _tiles(t, ctx_rows):
    return _Tiles(adaln_cols=1024, inproj_rows=min(1024, t), inproj_ctx_rows=min(512, ctx_rows),
                  inproj_cols=1024, attn_queries=min(2048, t), outproj_rows=min(512, t),
                  route_tokens=min(256, t), peer_tokens=min(512, t), peer_experts=1024)


def _rope_tables(t):
    rows = t // GRID_W
    row = jnp.repeat(jnp.arange(rows, dtype=F32), GRID_W)
    colp = jnp.tile(jnp.arange(GRID_W, dtype=F32), rows)
    n_freq = DA_HEAD_DIM // 4
    inv_freq = ROPE_BASE ** (-jnp.arange(n_freq, dtype=F32) / n_freq)
    ar = row[:, None] * inv_freq[None, :]
    ac = colp[:, None] * inv_freq[None, :]
    cos64 = jnp.concatenate([jnp.cos(ar), jnp.cos(ar), jnp.cos(ac), jnp.cos(ac)], axis=-1)
    sin64 = jnp.concatenate([-jnp.sin(ar), jnp.sin(ar), -jnp.sin(ac), jnp.sin(ac)], axis=-1)
    return jnp.tile(cos64, (1, 2)), jnp.tile(sin64, (1, 2))


def _layer(x, c, ctx, c_ctx, w_ada, b_ada, norm1_g, w_in, lb, hg_norm_g, lam, lam_init, da_subln_g,
           w_out, norm2_g, peer_w_query, peer_sub_keys, peer_u, peer_v, final_norm_g):
    b, t, d = x.shape
    tc = ctx.shape[1]
    hg_width = hg_norm_g.shape[0]
    hg_heads = hg_width // HG_EXPAND
    da_heads = da_subln_g.shape[0] // DA_V_DIM
    tiles = _tiles(t, b * tc)

    cond = jnp.zeros((8, d), F32).at[0:b].set(c).at[b].set(c_ctx)
    mod = _adaln(cond, w_ada, b_ada, tiles.adaln_cols)
    sh1, sc1, g1, sh2, sc2, g2 = [m[0:b].reshape(b, 1, d) for m in jnp.split(mod, 6, axis=-1)]
    csh1, csc1 = [m[b:b + 1].reshape(1, 1, d) for m in jnp.split(mod, 6, axis=-1)[0:2]]

    w_in_bf = w_in.astype(BF16)
    n_cols = w_in.shape[1]
    g1n = norm1_g.reshape(1, d)
    parts = _inproj(x.reshape(b * t, d), sc1, sh1, g1n, w_in_bf, t,
                    tiles.inproj_rows, tiles.inproj_cols).reshape(b, t, n_cols)
    cparts = _inproj(ctx.reshape(b * tc, d), csc1, csh1, g1n, w_in_bf, b * tc,
                     tiles.inproj_ctx_rows, tiles.inproj_cols).reshape(b, tc, n_cols)

    hg = _hgrn(cparts, parts, lb, hg_norm_g, hg_heads)
    cos_t, sin_t = _rope_tables(t)
    lam2 = jnp.stack([lam, jnp.asarray(1.0 - lam_init, F32)]).astype(F32)
    da = _diff_attn(cparts, parts, lam2, da_subln_g, cos_t, sin_t, da_heads, 5 * hg_heads, tiles.attn_queries)

    x2 = x.reshape(b * t, d)
    x1, h2 = _outproj(hg.reshape(b * t, -1), da.reshape(b * t, -1), x2, w_out.astype(BF16),
                      g1, sc2, sh2, norm2_g.reshape(1, d), t, tiles.outproj_rows)

    sk = peer_sub_keys.reshape(2 * PEER_HEADS, PEER_NKEYS, -1).astype(BF16)
    n1, a, r2, bb = _route(h2, peer_w_query.astype(BF16), sk, tiles.route_tokens)
    out = _peer(x1, g2, final_norm_g.reshape(1, d), h2, n1, a, r2, bb,
                peer_u.astype(BF16), peer_v.astype(BF16), t, tiles.peer_tokens, tiles.peer_experts)
    return out.reshape(b, t, d)


def kernel(x, c, ctx, c_ctx, w_ada, b_ada, norm1_g, w_in, hg_gamma, hg_norm_g, da_lambda_q1, da_lambda_k1,
           da_lambda_q2, da_lambda_k2, da_subln_g, w_out, norm2_g, peer_w_query, peer_sub_keys, peer_u,
           peer_v, final_norm_g):
    depth = w_ada.shape[0]
    assert depth == 1, "single-layer block: the context stream is never updated"
    lb_all = jnp.cumsum(jax.nn.softmax(hg_gamma.astype(F32), axis=0), axis=0)
    l = 0
    lam_init = 0.8 - 0.6 * math.exp(-0.3 * l)
    lam = (jnp.exp(jnp.sum(da_lambda_q1[l].astype(F32) * da_lambda_k1[l].astype(F32)))
           - jnp.exp(jnp.sum(da_lambda_q2[l].astype(F32) * da_lambda_k2[l].astype(F32))) + lam_init)
    return _layer(x, c, ctx, c_ctx, w_ada[l], b_ada[l], norm1_g[l], w_in[l], lb_all[l], hg_norm_g[l],
                  lam, lam_init, da_subln_g[l], w_out[l], norm2_g[l], peer_w_query[l], peer_sub_keys[l],
                  peer_u[l], peer_v[l], final_norm_g)
```

```python
import math
from typing import NamedTuple

import jax
import jax.numpy as jnp
from jax import lax
from jax.experimental import pallas as pl
from jax.experimental.pallas import tpu as pltpu

F32 = jnp.float32
BF16 = jnp.bfloat16

GRID_W = 64
HG_EXPAND = 128
HG_CHUNK = 64
HG_UNROLL = 4
DA_HEAD_DIM = 64
DA_V_DIM = 128
DA_Q_SUB = 256
ROPE_BASE = 10000.0
PEER_HEADS = 8
PEER_NKEYS = 128
PEER_TOPK = 16
RMS_EPS = 1e-6
LANES = 128
BF16_SUBLANES = 16
EXP_CLAMP = 80.0
V7X_VMEM_BYTES = 64 * 1024 * 1024
VMEM_LIMIT = V7X_VMEM_BYTES - 8 * 1024 * 1024
NEG_INF = float("-inf")


def _nt(a, b):
    return lax.dot_general(a, b, (((1,), (1,)), ((), ())), preferred_element_type=F32)


def _tn(a, b):
    return lax.dot_general(a, b, (((0,), (0,)), ((), ())), preferred_element_type=F32)


def _nn(a, b):
    return jnp.dot(a, b, preferred_element_type=F32)


def _cparams(sem):
    return pltpu.CompilerParams(dimension_semantics=sem, vmem_limit_bytes=VMEM_LIMIT)


def _adaln_kernel(c_ref, w_ref, b_ref, o_ref):
    c = c_ref[...]
    a = (c * jax.nn.sigmoid(c)).astype(BF16)
    o_ref[...] = _nn(a, w_ref[...].astype(BF16)) + b_ref[...]


def _adaln(cond, w, b, tn):
    d, n = w.shape
    return pl.pallas_call(
        _adaln_kernel,
        grid=(n // tn,),
        in_specs=[pl.BlockSpec((8, d), lambda j: (0, 0)),
                  pl.BlockSpec((d, tn), lambda j: (0, j)),
                  pl.BlockSpec((1, tn), lambda j: (0, j))],
        out_specs=pl.BlockSpec((8, tn), lambda j: (0, j)),
        out_shape=jax.ShapeDtypeStruct((8, n), F32),
        compiler_params=_cparams(("arbitrary",)),
        name="adaln",
    )(cond, w, b.reshape(1, n))


def _inproj_kernel(x_ref, sc_ref, sh_ref, g_ref, w_ref, o_ref, h_scr):
    @pl.when(pl.program_id(1) == 0)
    def _():
        x = x_ref[...]
        y = x * lax.rsqrt(jnp.mean(x * x, axis=-1, keepdims=True) + RMS_EPS) * g_ref[...]
        h_scr[...] = (y * (1.0 + sc_ref[0]) + sh_ref[0]).astype(BF16)

    o_ref[...] = _nn(h_scr[...], w_ref[...]).astype(BF16)


def _inproj(x2, sc, sh, g, w, rows_per_mod, tm, tn):
    m, d = x2.shape
    n = w.shape[1]
    bpm = rows_per_mod // tm
    return pl.pallas_call(
        _inproj_kernel,
        grid=(m // tm, n // tn),
        in_specs=[pl.BlockSpec((tm, d), lambda i, j: (i, 0)),
                  pl.BlockSpec((1, 1, d), lambda i, j: (i // bpm, 0, 0)),
                  pl.BlockSpec((1, 1, d), lambda i, j: (i // bpm, 0, 0)),
                  pl.BlockSpec((1, d), lambda i, j: (0, 0)),
                  pl.BlockSpec((d, tn), lambda i, j: (0, j))],
        out_specs=pl.BlockSpec((tm, tn), lambda i, j: (i, j)),
        out_shape=jax.ShapeDtypeStruct((m, n), BF16),
        scratch_shapes=[pltpu.VMEM((tm, d), BF16)],
        compiler_params=_cparams(("parallel", "arbitrary")),
        name="inproj",
    )(x2, sc, sh, g, w)


def _hgrn_gates(z, lb, tri):
    f = lb + (1.0 - lb) * jax.nn.sigmoid(z)
    lf = jnp.log(f)
    hi = lf.astype(BF16)
    lo = (lf - hi.astype(F32)).astype(BF16)
    both = _nn(tri, jnp.concatenate([hi, lo], axis=1))
    w = lf.shape[1]
    return 1.0 - f, both[:, 0:w] + both[:, w:2 * w]


def _hgrn_masks(rev):
    n = 2 * HG_CHUNK
    row = lax.broadcasted_iota(jnp.int32, (n, n), 0)
    col = lax.broadcasted_iota(jnp.int32, (n, n), 1)
    same = (row >= HG_CHUNK) == (col >= HG_CHUNK)
    before = (col >= row) if rev else (row >= col)
    diag = same & before
    cross = (~same) & before
    return jnp.where(diag, 1.0, 0.0).astype(BF16), diag, cross


def _hgrn_block(st, z, v_bf, q, lb, masks, rev):
    c = HG_CHUNK
    tri, diag, cross = masks
    k, cum = _hgrn_gates(z, lb, tri)
    second = lax.broadcasted_iota(jnp.int32, cum.shape, 0) < c if rev else \
        lax.broadcasted_iota(jnp.int32, cum.shape, 0) >= c
    tot_lo = cum[0:1, :] if rev else cum[c - 1:c, :]
    tot_hi = cum[c:c + 1, :] if rev else cum[2 * c - 1:2 * c, :]
    tot_first, tot_second = (tot_hi, tot_lo) if rev else (tot_lo, tot_hi)
    tot_own = jnp.where(lax.broadcasted_iota(jnp.int32, cum.shape, 0) < c, tot_lo, tot_hi)
    k_st = k * jnp.exp(tot_own - cum)
    k_end = k_st * jnp.where(second, 1.0, jnp.exp(tot_second))
    st_new = st * jnp.exp(tot_first + tot_second) + _tn(v_bf, k_end.astype(BF16))
    if q is None:
        return st_new, None, None, None
    mid_lo = cum[c // 2:c // 2 + 1, :] if rev else cum[c // 2 - 1:c // 2, :]
    mid_hi = cum[c + c // 2:c + c // 2 + 1, :] if rev else cum[c + c // 2 - 1:c + c // 2, :]
    rel = cum - jnp.where(lax.broadcasted_iota(jnp.int32, cum.shape, 0) < c, mid_lo, mid_hi)
    qt = q * jnp.exp(jnp.minimum(rel, EXP_CLAMP))
    kt = k * jnp.exp(jnp.minimum(-rel, EXP_CLAMP))
    q_in = q * jnp.exp(cum)
    scores = (jnp.where(diag, _nt(qt.astype(BF16), kt.astype(BF16)), 0.0)
              + jnp.where(cross, _nt(q_in.astype(BF16), k_st.astype(BF16)), 0.0))
    intra = _nn(scores.astype(BF16), v_bf)
    q_start = q_in * jnp.where(second, jnp.exp(tot_first), 1.0)
    inter = _nt(q_start.astype(BF16), st.astype(BF16))
    return st_new, intra, inter, jnp.abs(rel)


def _hgrn_intra_exact(z, v_bf, q, lb, masks, rev, cum_scr, k_scr):
    c = HG_CHUNK
    n = 2 * c
    tri, diag, cross = masks
    k, cum = _hgrn_gates(z, lb, tri)
    rows = lax.broadcasted_iota(jnp.int32, cum.shape, 0)
    if rev:
        cum = cum + jnp.where(rows < c, cum[c:c + 1, :], 0.0)
    else:
        cum = cum + jnp.where(rows >= c, cum[c - 1:c, :], 0.0)
    col = lax.broadcasted_iota(jnp.int32, (n, n), 1)
    cum_scr[...] = cum
    k_scr[...] = k

    def key_column(s, a):
        d = jnp.exp(jnp.minimum(cum - cum_scr[pl.ds(s, 1), :], 0.0))
        return jnp.where(col == s, jnp.sum(q * d * k_scr[pl.ds(s, 1), :], axis=-1, keepdims=True), a)

    a = lax.fori_loop(0, n, key_column, jnp.zeros((n, n), F32))
    return _nn(jnp.where(diag | cross, a, 0.0).astype(BF16), v_bf)


def _silu(x):
    return x * jax.nn.sigmoid(x)


def _hgrn_kernel(lb_ref, ng_ref, czf_ref, czb_ref, ci_ref, q_ref, zf_ref, zb_ref, i_ref, g_ref,
                 o_ref, of_scr, ob_scr, cum_scr, k_scr):
    n = 2 * HG_CHUNK
    n_ctx = czf_ref.shape[0] // n
    n_lat = q_ref.shape[0] // n
    un = math.gcd(HG_UNROLL, n_lat)
    masks_f = _hgrn_masks(False)
    masks_b = _hgrn_masks(True)
    lb_f = lb_ref[0:1, :]
    lb_b = lb_ref[1:2, :]

    def rows_of(block):
        return pl.ds(pl.multiple_of(block * n, n), n)

    def ctx_body(j, carry):
        sf, sb = carry
        rf = rows_of(j)
        rb = rows_of(n_ctx - 1 - j)
        sf = _hgrn_block(sf, czf_ref[rf, :].astype(F32), ci_ref[rf, :], None, lb_f, masks_f, False)[0]
        sb = _hgrn_block(sb, czb_ref[rb, :].astype(F32), ci_ref[rb, :], None, lb_b, masks_b, True)[0]
        return sf, sb

    s0 = jnp.zeros((HG_EXPAND, HG_EXPAND), F32)
    sf, sb = lax.fori_loop(0, n_ctx, ctx_body, (s0, s0))

    def lat_body(j, carry):
        sf, sb = carry
        worst = None
        redo = []
        for u in range(un):
            rf = rows_of(j * un + u)
            rb = rows_of(n_lat - 1 - (j * un + u))
            sf, intra_f, inter_f, ex_f = _hgrn_block(sf, zf_ref[rf, :].astype(F32), i_ref[rf, :],
                                                     _silu(q_ref[rf, :].astype(F32)), lb_f, masks_f, False)
            sb, intra_b, inter_b, ex_b = _hgrn_block(sb, zb_ref[rb, :].astype(F32), i_ref[rb, :],
                                                     _silu(q_ref[rb, :].astype(F32)), lb_b, masks_b, True)
            of_scr[rf, :] = intra_f + inter_f
            ob_scr[rb, :] = intra_b + inter_b
            ex = jnp.maximum(ex_f, ex_b)
            worst = ex if worst is None else jnp.maximum(worst, ex)
            redo.append((rf, rb, inter_f, inter_b))

        @pl.when(jnp.max(worst) > EXP_CLAMP)
        def _():
            for rf, rb, inter_f, inter_b in redo:
                of_scr[rf, :] = inter_f + _hgrn_intra_exact(
                    zf_ref[rf, :].astype(F32), i_ref[rf, :], _silu(q_ref[rf, :].astype(F32)), lb_f, masks_f, False,
                    cum_scr, k_scr)
                ob_scr[rb, :] = inter_b + _hgrn_intra_exact(
                    zb_ref[rb, :].astype(F32), i_ref[rb, :], _silu(q_ref[rb, :].astype(F32)), lb_b, masks_b, True,
                    cum_scr, k_scr)

        return sf, sb

    lax.fori_loop(0, n_lat // un, lat_body, (sf, sb))

    rows = 256
    ng = ng_ref[...]

    def out_body(j, _):
        r = pl.ds(pl.multiple_of(j * rows, rows), rows)
        o = of_scr[r, :] + ob_scr[r, :]
        o = o * lax.rsqrt(jnp.mean(o * o, axis=-1, keepdims=True) + RMS_EPS) * ng
        o_ref[r, :] = (o * _silu(g_ref[r, :].astype(F32))).astype(BF16)
        return 0

    lax.fori_loop(0, q_ref.shape[0] // rows, out_body, 0)


def _hgrn(cparts, parts, lb, norm_g, n_heads):
    b, t, _ = parts.shape
    tc = cparts.shape[1]
    w = HG_EXPAND

    def col(part):
        return lambda bi, hi: (bi, 0, part * n_heads + hi)

    cspec = lambda part: pl.BlockSpec((None, tc, w), col(part))
    lspec = lambda part: pl.BlockSpec((None, t, w), col(part))
    return pl.pallas_call(
        _hgrn_kernel,
        grid=(b, n_heads),
        in_specs=[pl.BlockSpec((2, w), lambda bi, hi: (0, hi)),
                  pl.BlockSpec((1, w), lambda bi, hi: (0, hi)),
                  cspec(1), cspec(2), cspec(3),
                  lspec(0), lspec(1), lspec(2), lspec(3), lspec(4)],
        out_specs=pl.BlockSpec((None, t, w), lambda bi, hi: (bi, 0, hi)),
        out_shape=jax.ShapeDtypeStruct((b, t, n_heads * w), BF16),
        scratch_shapes=[pltpu.VMEM((t, w), F32), pltpu.VMEM((t, w), F32),
                        pltpu.VMEM((2 * HG_CHUNK, w), F32), pltpu.VMEM((2 * HG_CHUNK, w), F32)],
        compiler_params=_cparams(("parallel", "parallel")),
        name="hgrn2",
    )(lb, norm_g.reshape(1, -1), cparts, cparts, cparts, parts, parts, parts, parts, parts)


def _rope(x, cos, sin):
    lane = lax.broadcasted_iota(jnp.int32, x.shape, 1)
    swapped = jnp.where((lane % 32) < 16,
                        pltpu.roll(x, LANES - 16, axis=1),
                        pltpu.roll(x, 16, axis=1))
    return x * cos + swapped * sin


def _da_kernel(lam_ref, q_ref, kc_ref, kl_ref, vc_ref, vl_ref, cq_ref, sq_ref, ck_ref, sk_ref, g_ref,
               o_ref, k_scr, v_scr):
    tc = kc_ref.shape[0]
    t = kl_ref.shape[0]
    dv = DA_V_DIM

    @pl.when(pl.program_id(2) == 0)
    def _():
        lane = lax.broadcasted_iota(jnp.int32, (tc + t, LANES), 1)
        v_scr[:, dv:dv + LANES] = jnp.where(lane == 0, 1.0, 0.0).astype(BF16)
        k_scr[:, 0:tc] = kc_ref[...].astype(F32).T.astype(BF16)
        v_scr[0:tc, 0:dv] = vc_ref[...]
        v_scr[tc:tc + t, 0:dv] = vl_ref[...]
        rows = min(512, t)
        for j in range(t // rows):
            r = slice(j * rows, (j + 1) * rows)
            kr = _rope(kl_ref[r, :].astype(F32), ck_ref[r, :], sk_ref[r, :])
            k_scr[:, tc + j * rows:tc + (j + 1) * rows] = kr.T.astype(BF16)

    lam = lam_ref[0]
    out_scale = lam_ref[1]
    qscale = (DA_HEAD_DIM ** -0.5) * math.log2(math.e)
    k = k_scr[...]
    v1 = v_scr[...]

    def branch(qm):
        s = _nn(qm.astype(BF16), k)
        p = jnp.exp2(s - jnp.max(s, axis=-1, keepdims=True)).astype(BF16)
        ov = _nn(p, v1)
        return ov[:, 0:dv] / ov[:, dv:dv + 1]

    sub = min(DA_Q_SUB, q_ref.shape[0])
    for j in range(q_ref.shape[0] // sub):
        r = slice(j * sub, (j + 1) * sub)
        qr = _rope(q_ref[r, :].astype(F32), cq_ref[r, :], sq_ref[r, :]) * qscale
        lane = lax.broadcasted_iota(jnp.int32, qr.shape, 1)
        o = (branch(jnp.where(lane < DA_HEAD_DIM, qr, 0.0))
             - lam * branch(jnp.where(lane >= DA_HEAD_DIM, qr, 0.0)))
        o = o * lax.rsqrt(jnp.mean(o * o, axis=-1, keepdims=True) + RMS_EPS)
        o_ref[r, :] = (o * g_ref[...] * out_scale).astype(BF16)


def _diff_attn(cparts, parts, lam2, subln_g, cos_t, sin_t, n_heads, col0, tq):
    b, t, _ = parts.shape
    tc = cparts.shape[1]
    w = LANES

    def col(part):
        return lambda bi, hi, qi: (bi, 0, col0 + part * n_heads + hi)

    return pl.pallas_call(
        _da_kernel,
        grid=(b, n_heads, t // tq),
        in_specs=[pl.BlockSpec(memory_space=pltpu.SMEM),
                  pl.BlockSpec((None, tq, w), lambda bi, hi, qi: (bi, qi, col0 + hi)),
                  pl.BlockSpec((None, tc, w), col(1)),
                  pl.BlockSpec((None, t, w), col(1)),
                  pl.BlockSpec((None, tc, w), col(2)),
                  pl.BlockSpec((None, t, w), col(2)),
                  pl.BlockSpec((tq, w), lambda bi, hi, qi: (qi, 0)),
                  pl.BlockSpec((tq, w), lambda bi, hi, qi: (qi, 0)),
                  pl.BlockSpec((t, w), lambda bi, hi, qi: (0, 0)),
                  pl.BlockSpec((t, w), lambda bi, hi, qi: (0, 0)),
                  pl.BlockSpec((1, w), lambda bi, hi, qi: (0, hi))],
        out_specs=pl.BlockSpec((None, tq, w), lambda bi, hi, qi: (bi, qi, hi)),
        out_shape=jax.ShapeDtypeStruct((b, t, n_heads * w), BF16),
        scratch_shapes=[pltpu.VMEM((w, tc + t), BF16), pltpu.VMEM((tc + t, DA_V_DIM + w), BF16)],
        compiler_params=_cparams(("parallel", "parallel", "arbitrary")),
        name="diff_attn",
    )(lam2, parts, cparts, parts, cparts, parts, cos_t, sin_t, cos_t, sin_t, subln_g.reshape(1, -1))


def _outproj_kernel(hg_ref, da_ref, x_ref, wa_ref, wb_ref, g1_ref, sc_ref, sh_ref, ng_ref, x1_ref, h2_ref):
    mix = _nn(hg_ref[...], wa_ref[...]) + _nn(da_ref[...], wb_ref[...])
    x1 = x_ref[...] + g1_ref[0] * mix
    x1_ref[...] = x1
    y = x1 * lax.rsqrt(jnp.mean(x1 * x1, axis=-1, keepdims=True) + RMS_EPS) * ng_ref[...]
    h2_ref[...] = (y * (1.0 + sc_ref[0]) + sh_ref[0]).astype(BF16)


def _outproj(hg, da, x2, w_out, g1, sc2, sh2, norm_g, rows_per_mod, tm):
    m, d = x2.shape
    half = hg.shape[1]
    bpm = rows_per_mod // tm
    mod = pl.BlockSpec((1, 1, d), lambda i: (i // bpm, 0, 0))
    return pl.pallas_call(
        _outproj_kernel,
        grid=(m // tm,),
        in_specs=[pl.BlockSpec((tm, half), lambda i: (i, 0)),
                  pl.BlockSpec((tm, half), lambda i: (i, 0)),
                  pl.BlockSpec((tm, d), lambda i: (i, 0)),
                  pl.BlockSpec((half, d), lambda i: (0, 0)),
                  pl.BlockSpec((half, d), lambda i: (1, 0)),
                  mod, mod, mod,
                  pl.BlockSpec((1, d), lambda i: (0, 0))],
        out_specs=[pl.BlockSpec((tm, d), lambda i: (i, 0)),
                   pl.BlockSpec((tm, d), lambda i: (i, 0))],
        out_shape=[jax.ShapeDtypeStruct((m, d), F32), jax.ShapeDtypeStruct((m, d), BF16)],
        compiler_params=_cparams(("parallel",)),
        name="outproj",
    )(hg, da, x2, w_out, w_out, g1, sc2, sh2, norm_g)


def _top_values(s, n):
    vals = []
    rank = jnp.full(s.shape, float(n), F32)
    for r in range(n):
        m = jnp.max(s, axis=0, keepdims=True)
        hit = s == m
        vals.append(m)
        rank = jnp.where(hit, float(r), rank)
        s = jnp.where(hit, NEG_INF, s)
    return vals, rank


def _route_tables(s1, s2):
    kk = PEER_TOPK
    v1, rank1 = _top_values(s1, kk)
    v2, rank2 = _top_values(s2, kk)
    sv2 = jnp.concatenate(v2, axis=0)
    cands = [v1[0] + sv2]
    cands += [v1[r] + sv2[0:8, :] for r in range(1, 8)]
    cands.append(jnp.concatenate(v1[8:16], axis=0) + v2[0])
    top = v1[0] + v2[0]
    z = jnp.zeros_like(top)
    tau = top
    work = cands
    for _ in range(kk):
        m8 = work[1]
        for cnd in work[2:]:
            m8 = jnp.maximum(m8, cnd)
        tau = jnp.maximum(jnp.max(work[0], axis=0, keepdims=True), jnp.max(m8, axis=0, keepdims=True))
        z = z + jnp.exp(tau - top)
        work = [jnp.where(cnd == tau, NEG_INF, cnd) for cnd in work]
    cnt = [jnp.sum(jnp.where(cands[r] >= tau, 1.0, 0.0), axis=0, keepdims=True) for r in range(8)]
    tail = jnp.where(cands[8] >= tau, 1.0, 0.0)
    cnt += [tail[r:r + 1, :] for r in range(8)]
    n1 = jnp.zeros_like(s1)
    for r in range(kk):
        n1 = jnp.where(rank1 == float(r), cnt[r], n1)
    return n1, jnp.exp(s1 - v1[0]), rank2.astype(BF16), (jnp.exp(s2 - v2[0]) / z).astype(BF16)


def _route_kernel(h_ref, wq_ref, sk_ref, n1_ref, a_ref, r2_ref, b_ref):
    tp = h_ref.shape[0]
    q = _nn(h_ref[...], wq_ref[...]).astype(BF16)
    for h in range(PEER_HEADS):
        s1 = _nt(sk_ref[2 * h], q[:, (2 * h) * LANES:(2 * h + 1) * LANES])
        s2 = _nt(sk_ref[2 * h + 1], q[:, (2 * h + 1) * LANES:(2 * h + 2) * LANES])
        for c in range(tp // LANES):
            lanes = slice(c * LANES, (c + 1) * LANES)
            n1, a, r2, b = _route_tables(s1[:, lanes], s2[:, lanes])
            n1_ref[h, :, lanes] = n1
            a_ref[h, :, lanes] = a
            r2_ref[h, :, lanes] = r2
            b_ref[h, :, lanes] = b


def _route(h2, wq, sk, tp):
    p, d = h2.shape
    nq = wq.shape[1]
    tab = lambda: pl.BlockSpec((PEER_HEADS, PEER_NKEYS, tp), lambda i: (0, 0, i))
    return pl.pallas_call(
        _route_kernel,
        grid=(p // tp,),
        in_specs=[pl.BlockSpec((tp, d), lambda i: (i, 0)),
                  pl.BlockSpec((d, nq), lambda i: (0, 0)),
                  pl.BlockSpec((2 * PEER_HEADS, PEER_NKEYS, LANES), lambda i: (0, 0, 0))],
        out_specs=[tab(), tab(), tab(), tab()],
        out_shape=[jax.ShapeDtypeStruct((PEER_HEADS, PEER_NKEYS, p), F32),
                   jax.ShapeDtypeStruct((PEER_HEADS, PEER_NKEYS, p), F32),
                   jax.ShapeDtypeStruct((PEER_HEADS, PEER_NKEYS, p), BF16),
                   jax.ShapeDtypeStruct((PEER_HEADS, PEER_NKEYS, p), BF16)],
        compiler_params=_cparams(("parallel",)),
        name="peer_route",
    )(h2, wq, sk)


PEER_SUB = 512
PEER_LANE_CHUNK = 256


def _peer_kernel(x1_ref, g2_ref, fg_ref, h_ref, n1_ref, a_ref, r2_ref, b_ref, u_ref, v_ref, o_ref,
                 acc, ht_scr, act_a, act_b):
    e = pl.program_id(1)
    te, tp = ht_scr.shape
    n_slab = te // PEER_NKEYS
    slabs_per_sub = PEER_SUB // PEER_NKEYS
    zero = jnp.zeros((), BF16)

    @pl.when(e == 0)
    def _():
        acc[...] = jnp.zeros_like(acc)

    n_sub = te // PEER_SUB
    act_bufs = (act_a, act_b)

    def first_matmul(sub):
        act_bufs[sub % 2][...] = _nt(u_ref[sub * PEER_SUB:(sub + 1) * PEER_SUB, :], h_ref[...])

    first_matmul(0)
    for sub in range(n_sub):
        if sub + 1 < n_sub:
            first_matmul(sub + 1)
        act = act_bufs[sub % 2]
        for s in range(slabs_per_sub):
            sl = sub * slabs_per_sub + s
            i1 = e * n_slab + sl
            n1rows = [n1_ref[h, pl.ds(i1, 1), :] for h in range(PEER_HEADS)]
            arows = [a_ref[h, pl.ds(i1, 1), :] for h in range(PEER_HEADS)]
            for lc in range(tp // PEER_LANE_CHUNK):
                lanes = slice(lc * PEER_LANE_CHUNK, (lc + 1) * PEER_LANE_CHUNK)
                tile = (BF16_SUBLANES, PEER_LANE_CHUNK)
                n1b = [jnp.broadcast_to(r[:, lanes], tile).astype(BF16) for r in n1rows]
                ab = [jnp.broadcast_to(r[:, lanes], tile).astype(BF16) for r in arows]
                for g in range(PEER_NKEYS // BF16_SUBLANES):
                    keys = slice(g * BF16_SUBLANES, (g + 1) * BF16_SUBLANES)
                    gate = None
                    for h in range(PEER_HEADS):
                        term = jnp.where(r2_ref[h, keys, lanes] < n1b[h], b_ref[h, keys, lanes], zero) * ab[h]
                        gate = term if gate is None else gate + term
                    rows = slice(s * PEER_NKEYS + g * BF16_SUBLANES, s * PEER_NKEYS + (g + 1) * BF16_SUBLANES)
                    a = act[rows, lanes].astype(BF16)
                    gelu = 0.5 * a * (1.0 + lax.erf(a * math.sqrt(0.5)))
                    rows = slice(sl * PEER_NKEYS + g * BF16_SUBLANES, sl * PEER_NKEYS + (g + 1) * BF16_SUBLANES)
                    ht_scr[rows, lanes] = gate * gelu
    acc[...] += _tn(ht_scr[...], v_ref[...])

    @pl.when(e == pl.num_programs(1) - 1)
    def _():
        x = x1_ref[...] + g2_ref[0] * acc[...]
        o_ref[...] = x * lax.rsqrt(jnp.mean(x * x, axis=-1, keepdims=True) + RMS_EPS) * fg_ref[...]


def _peer(x1, g2, fg, h2, n1, a, r2, bb, u_bf, v_bf, rows_per_mod, tp, te):
    p, d = x1.shape
    n_exp = u_bf.shape[0]
    bpm = rows_per_mod // tp
    tab = lambda: pl.BlockSpec((PEER_HEADS, PEER_NKEYS, tp), lambda i, e: (0, 0, i))
    return pl.pallas_call(
        _peer_kernel,
        grid=(p // tp, n_exp // te),
        in_specs=[pl.BlockSpec((tp, d), lambda i, e: (i, 0), pipeline_mode=pl.Buffered(1)),
                  pl.BlockSpec((1, 1, d), lambda i, e: (i // bpm, 0, 0)),
                  pl.BlockSpec((1, d), lambda i, e: (0, 0)),
                  pl.BlockSpec((tp, d), lambda i, e: (i, 0), pipeline_mode=pl.Buffered(1)),
                  tab(), tab(), tab(), tab(),
                  pl.BlockSpec((te, d), lambda i, e: (e, 0)),
                  pl.BlockSpec((te, d), lambda i, e: (e, 0))],
        out_specs=pl.BlockSpec((tp, d), lambda i, e: (i, 0)),
        out_shape=jax.ShapeDtypeStruct((p, d), F32),
        scratch_shapes=[pltpu.VMEM((tp, d), F32),
                        pltpu.VMEM((te, tp), BF16),
                        pltpu.VMEM((PEER_SUB, tp), F32),
                        pltpu.VMEM((PEER_SUB, tp), F32)],
        compiler_params=_cparams(("parallel", "arbitrary")),
        name="peer_experts",
    )(x1, g2, fg, h2, n1, a, r2, bb, u_bf, v_bf)


class _Tiles(NamedTuple):
    adaln_cols: int
    inproj_rows: int
    inproj_ctx_rows: int
    inproj_cols: int
    attn_queries: int
    outproj_rows: int
    route_tokens: int
    peer_tokens: int
    peer_experts: int


def _tiles(t, ctx_rows):
    return _Tiles(adaln_cols=1024, inproj_rows=min(1024, t), inproj_ctx_rows=min(512, ctx_rows),
                  inproj_cols=1024, attn_queries=min(4096, t), outproj_rows=min(512, t),
                  route_tokens=min(256, t), peer_tokens=min(512, t), peer_experts=1024)


def _rope_tables(t):
    rows = t // GRID_W
    row = jnp.repeat(jnp.arange(rows, dtype=F32), GRID_W)
    colp = jnp.tile(jnp.arange(GRID_W, dtype=F32), rows)
    n_freq = DA_HEAD_DIM // 4
    inv_freq = ROPE_BASE ** (-jnp.arange(n_freq, dtype=F32) / n_freq)
    ar = row[:, None] * inv_freq[None, :]
    ac = colp[:, None] * inv_freq[None, :]
    cos64 = jnp.concatenate([jnp.cos(ar), jnp.cos(ar), jnp.cos(ac), jnp.cos(ac)], axis=-1)
    sin64 = jnp.concatenate([-jnp.sin(ar), jnp.sin(ar), -jnp.sin(ac), jnp.sin(ac)], axis=-1)
    return jnp.tile(cos64, (1, 2)), jnp.tile(sin64, (1, 2))


def _layer(x, c, ctx, c_ctx, w_ada, b_ada, norm1_g, w_in, lb, hg_norm_g, lam, lam_init, da_subln_g,
           w_out, norm2_g, peer_w_query, peer_sub_keys, peer_u, peer_v, final_norm_g):
    b, t, d = x.shape
    tc = ctx.shape[1]
    hg_width = hg_norm_g.shape[0]
    hg_heads = hg_width // HG_EXPAND
    da_heads = da_subln_g.shape[0] // DA_V_DIM
    tiles = _tiles(t, b * tc)

    cond = jnp.zeros((8, d), F32).at[0:b].set(c).at[b].set(c_ctx)
    mod = _adaln(cond, w_ada, b_ada, tiles.adaln_cols)
    sh1, sc1, g1, sh2, sc2, g2 = [m[0:b].reshape(b, 1, d) for m in jnp.split(mod, 6, axis=-1)]
    csh1, csc1 = [m[b:b + 1].reshape(1, 1, d) for m in jnp.split(mod, 6, axis=-1)[0:2]]

    w_in_bf = w_in.astype(BF16)
    n_cols = w_in.shape[1]
    g1n = norm1_g.reshape(1, d)
    parts = _inproj(x.reshape(b * t, d), sc1, sh1, g1n, w_in_bf, t,
                    tiles.inproj_rows, tiles.inproj_cols).reshape(b, t, n_cols)
    cparts = _inproj(ctx.reshape(b * tc, d), csc1, csh1, g1n, w_in_bf, b * tc,
                     tiles.inproj_ctx_rows, tiles.inproj_cols).reshape(b, tc, n_cols)

    hg = _hgrn(cparts, parts, lb, hg_norm_g, hg_heads)
    cos_t, sin_t = _rope_tables(t)
    lam2 = jnp.stack([lam, jnp.asarray(1.0 - lam_init, F32)]).astype(F32)
    da = _diff_attn(cparts, parts, lam2, da_subln_g, cos_t, sin_t, da_heads, 5 * hg_heads, tiles.attn_queries)

    x2 = x.reshape(b * t, d)
    x1, h2 = _outproj(hg.reshape(b * t, -1), da.reshape(b * t, -1), x2, w_out.astype(BF16),
                      g1, sc2, sh2, norm2_g.reshape(1, d), t, tiles.outproj_rows)

    sk = peer_sub_keys.reshape(2 * PEER_HEADS, PEER_NKEYS, -1).astype(BF16)
    n1, a, r2, bb = _route(h2, peer_w_query.astype(BF16), sk, tiles.route_tokens)
    out = _peer(x1, g2, final_norm_g.reshape(1, d), h2, n1, a, r2, bb,
                peer_u.astype(BF16), peer_v.astype(BF16), t, tiles.peer_tokens, tiles.peer_experts)
    return out.reshape(b, t, d)


def kernel(x, c, ctx, c_ctx, w_ada, b_ada, norm1_g, w_in, hg_gamma, hg_norm_g, da_lambda_q1, da_lambda_k1,
           da_lambda_q2, da_lambda_k2, da_subln_g, w_out, norm2_g, peer_w_query, peer_sub_keys, peer_u,
           peer_v, final_norm_g):
    depth = w_ada.shape[0]
    assert depth == 1, "single-layer block: the context stream is never updated"
    lb_all = jnp.cumsum(jax.nn.softmax(hg_gamma.astype(F32), axis=0), axis=0)
    l = 0
    lam_init = 0.8 - 0.6 * math.exp(-0.3 * l)
    lam = (jnp.exp(jnp.sum(da_lambda_q1[l].astype(F32) * da_lambda_k1[l].astype(F32)))
           - jnp.exp(jnp.sum(da_lambda_q2[l].astype(F32) * da_lambda_k2[l].astype(F32))) + lam_init)
    return _layer(x, c, ctx, c_ctx, w_ada[l], b_ada[l], norm1_g[l], w_in[l], lb_all[l], hg_norm_g[l],
                  lam, lam_init, da_subln_g[l], w_out[l], norm2_g[l], peer_w_query[l], peer_sub_keys[l],
                  peer_u[l], peer_v[l], final_norm_g)
```

```python
import math
from typing import NamedTuple

import jax
import jax.numpy as jnp
from jax import lax
from jax.experimental import pallas as pl
from jax.experimental.pallas import tpu as pltpu

F32 = jnp.float32
BF16 = jnp.bfloat16

GRID_W = 64
HG_EXPAND = 128
HG_CHUNK = 64
HG_UNROLL = 4
DA_HEAD_DIM = 64
DA_V_DIM = 128
DA_Q_SUB = 256
ROPE_BASE = 10000.0
PEER_HEADS = 8
PEER_NKEYS = 128
PEER_TOPK = 16
RMS_EPS = 1e-6
LANES = 128
BF16_SUBLANES = 16
EXP_CLAMP = 80.0
V7X_VMEM_BYTES = 64 * 1024 * 1024
VMEM_LIMIT = V7X_VMEM_BYTES - 8 * 1024 * 1024
NEG_INF = float("-inf")


def _nt(a, b):
    return lax.dot_general(a, b, (((1,), (1,)), ((), ())), preferred_element_type=F32)


def _tn(a, b):
    return lax.dot_general(a, b, (((0,), (0,)), ((), ())), preferred_element_type=F32)


def _nn(a, b):
    return jnp.dot(a, b, preferred_element_type=F32)


def _cparams(sem):
    return pltpu.CompilerParams(dimension_semantics=sem, vmem_limit_bytes=VMEM_LIMIT)


def _adaln_kernel(c_ref, w_ref, b_ref, o_ref):
    c = c_ref[...]
    a = (c * jax.nn.sigmoid(c)).astype(BF16)
    o_ref[...] = _nn(a, w_ref[...].astype(BF16)) + b_ref[...]


def _adaln(cond, w, b, tn):
    d, n = w.shape
    return pl.pallas_call(
        _adaln_kernel,
        grid=(n // tn,),
        in_specs=[pl.BlockSpec((8, d), lambda j: (0, 0)),
                  pl.BlockSpec((d, tn), lambda j: (0, j)),
                  pl.BlockSpec((1, tn), lambda j: (0, j))],
        out_specs=pl.BlockSpec((8, tn), lambda j: (0, j)),
        out_shape=jax.ShapeDtypeStruct((8, n), F32),
        compiler_params=_cparams(("arbitrary",)),
        name="adaln",
    )(cond, w, b.reshape(1, n))


def _inproj_kernel(x_ref, sc_ref, sh_ref, g_ref, w_ref, o_ref, h_scr):
    @pl.when(pl.program_id(1) == 0)
    def _():
        x = x_ref[...]
        y = x * lax.rsqrt(jnp.mean(x * x, axis=-1, keepdims=True) + RMS_EPS) * g_ref[...]
        h_scr[...] = (y * (1.0 + sc_ref[0]) + sh_ref[0]).astype(BF16)

    o_ref[...] = _nn(h_scr[...], w_ref[...]).astype(BF16)


def _inproj(x2, sc, sh, g, w, rows_per_mod, tm, tn):
    m, d = x2.shape
    n = w.shape[1]
    bpm = rows_per_mod // tm
    return pl.pallas_call(
        _inproj_kernel,
        grid=(m // tm, n // tn),
        in_specs=[pl.BlockSpec((tm, d), lambda i, j: (i, 0)),
                  pl.BlockSpec((1, 1, d), lambda i, j: (i // bpm, 0, 0)),
                  pl.BlockSpec((1, 1, d), lambda i, j: (i // bpm, 0, 0)),
                  pl.BlockSpec((1, d), lambda i, j: (0, 0)),
                  pl.BlockSpec((d, tn), lambda i, j: (0, j))],
        out_specs=pl.BlockSpec((tm, tn), lambda i, j: (i, j)),
        out_shape=jax.ShapeDtypeStruct((m, n), BF16),
        scratch_shapes=[pltpu.VMEM((tm, d), BF16)],
        compiler_params=_cparams(("parallel", "arbitrary")),
        name="inproj",
    )(x2, sc, sh, g, w)


def _hgrn_gates(z, lb, tri):
    f = lb + (1.0 - lb) * jax.nn.sigmoid(z)
    lf = jnp.log(f)
    hi = lf.astype(BF16)
    lo = (lf - hi.astype(F32)).astype(BF16)
    both = _nn(tri, jnp.concatenate([hi, lo], axis=1))
    w = lf.shape[1]
    return 1.0 - f, both[:, 0:w] + both[:, w:2 * w]


def _hgrn_masks(rev):
    n = 2 * HG_CHUNK
    row = lax.broadcasted_iota(jnp.int32, (n, n), 0)
    col = lax.broadcasted_iota(jnp.int32, (n, n), 1)
    same = (row >= HG_CHUNK) == (col >= HG_CHUNK)
    before = (col >= row) if rev else (row >= col)
    diag = same & before
    cross = (~same) & before
    return jnp.where(diag, 1.0, 0.0).astype(BF16), diag, cross


def _hgrn_block(st, z, v_bf, q, lb, masks, rev):
    c = HG_CHUNK
    tri, diag, cross = masks
    k, cum = _hgrn_gates(z, lb, tri)
    second = lax.broadcasted_iota(jnp.int32, cum.shape, 0) < c if rev else \
        lax.broadcasted_iota(jnp.int32, cum.shape, 0) >= c
    tot_lo = cum[0:1, :] if rev else cum[c - 1:c, :]
    tot_hi = cum[c:c + 1, :] if rev else cum[2 * c - 1:2 * c, :]
    tot_first, tot_second = (tot_hi, tot_lo) if rev else (tot_lo, tot_hi)
    tot_own = jnp.where(lax.broadcasted_iota(jnp.int32, cum.shape, 0) < c, tot_lo, tot_hi)
    k_st = k * jnp.exp(tot_own - cum)
    k_end = k_st * jnp.where(second, 1.0, jnp.exp(tot_second))
    st_new = st * jnp.exp(tot_first + tot_second) + _tn(v_bf, k_end.astype(BF16))
    if q is None:
        return st_new, None, None, None
    mid_lo = cum[c // 2:c // 2 + 1, :] if rev else cum[c // 2 - 1:c // 2, :]
    mid_hi = cum[c + c // 2:c + c // 2 + 1, :] if rev else cum[c + c // 2 - 1:c + c // 2, :]
    rel = cum - jnp.where(lax.broadcasted_iota(jnp.int32, cum.shape, 0) < c, mid_lo, mid_hi)
    qt = q * jnp.exp(jnp.minimum(rel, EXP_CLAMP))
    kt = k * jnp.exp(jnp.minimum(-rel, EXP_CLAMP))
    q_in = q * jnp.exp(cum)
    scores = (jnp.where(diag, _nt(qt.astype(BF16), kt.astype(BF16)), 0.0)
              + jnp.where(cross, _nt(q_in.astype(BF16), k_st.astype(BF16)), 0.0))
    intra = _nn(scores.astype(BF16), v_bf)
    q_start = q_in * jnp.where(second, jnp.exp(tot_first), 1.0)
    inter = _nt(q_start.astype(BF16), st.astype(BF16))
    return st_new, intra, inter, jnp.abs(rel)


def _hgrn_intra_exact(z, v_bf, q, lb, masks, rev, cum_scr, k_scr):
    c = HG_CHUNK
    n = 2 * c
    tri, diag, cross = masks
    k, cum = _hgrn_gates(z, lb, tri)
    rows = lax.broadcasted_iota(jnp.int32, cum.shape, 0)
    if rev:
        cum = cum + jnp.where(rows < c, cum[c:c + 1, :], 0.0)
    else:
        cum = cum + jnp.where(rows >= c, cum[c - 1:c, :], 0.0)
    col = lax.broadcasted_iota(jnp.int32, (n, n), 1)
    cum_scr[...] = cum
    k_scr[...] = k

    def key_column(s, a):
        d = jnp.exp(jnp.minimum(cum - cum_scr[pl.ds(s, 1), :], 0.0))
        return jnp.where(col == s, jnp.sum(q * d * k_scr[pl.ds(s, 1), :], axis=-1, keepdims=True), a)

    a = lax.fori_loop(0, n, key_column, jnp.zeros((n, n), F32))
    return _nn(jnp.where(diag | cross, a, 0.0).astype(BF16), v_bf)


def _silu(x):
    return x * jax.nn.sigmoid(x)


def _hgrn_kernel(lb_ref, ng_ref, czf_ref, czb_ref, ci_ref, q_ref, zf_ref, zb_ref, i_ref, g_ref,
                 o_ref, of_scr, ob_scr, cum_scr, k_scr):
    n = 2 * HG_CHUNK
    n_ctx = czf_ref.shape[0] // n
    n_lat = q_ref.shape[0] // n
    un = math.gcd(HG_UNROLL, n_lat)
    masks_f = _hgrn_masks(False)
    masks_b = _hgrn_masks(True)
    lb_f = lb_ref[0:1, :]
    lb_b = lb_ref[1:2, :]

    def rows_of(block):
        return pl.ds(pl.multiple_of(block * n, n), n)

    def ctx_body(j, carry):
        sf, sb = carry
        rf = rows_of(j)
        rb = rows_of(n_ctx - 1 - j)
        sf = _hgrn_block(sf, czf_ref[rf, :].astype(F32), ci_ref[rf, :], None, lb_f, masks_f, False)[0]
        sb = _hgrn_block(sb, czb_ref[rb, :].astype(F32), ci_ref[rb, :], None, lb_b, masks_b, True)[0]
        return sf, sb

    s0 = jnp.zeros((HG_EXPAND, HG_EXPAND), F32)
    sf, sb = lax.fori_loop(0, n_ctx, ctx_body, (s0, s0))

    def lat_body(j, carry):
        sf, sb = carry
        worst = None
        redo = []
        for u in range(un):
            rf = rows_of(j * un + u)
            rb = rows_of(n_lat - 1 - (j * un + u))
            sf, intra_f, inter_f, ex_f = _hgrn_block(sf, zf_ref[rf, :].astype(F32), i_ref[rf, :],
                                                     _silu(q_ref[rf, :].astype(F32)), lb_f, masks_f, False)
            sb, intra_b, inter_b, ex_b = _hgrn_block(sb, zb_ref[rb, :].astype(F32), i_ref[rb, :],
                                                     _silu(q_ref[rb, :].astype(F32)), lb_b, masks_b, True)
            of_scr[rf, :] = intra_f + inter_f
            ob_scr[rb, :] = intra_b + inter_b
            ex = jnp.maximum(ex_f, ex_b)
            worst = ex if worst is None else jnp.maximum(worst, ex)
            redo.append((rf, rb, inter_f, inter_b))

        @pl.when(jnp.max(worst) > EXP_CLAMP)
        def _():
            for rf, rb, inter_f, inter_b in redo:
                of_scr[rf, :] = inter_f + _hgrn_intra_exact(
                    zf_ref[rf, :].astype(F32), i_ref[rf, :], _silu(q_ref[rf, :].astype(F32)), lb_f, masks_f, False,
                    cum_scr, k_scr)
                ob_scr[rb, :] = inter_b + _hgrn_intra_exact(
                    zb_ref[rb, :].astype(F32), i_ref[rb, :], _silu(q_ref[rb, :].astype(F32)), lb_b, masks_b, True,
                    cum_scr, k_scr)

        return sf, sb

    lax.fori_loop(0, n_lat // un, lat_body, (sf, sb))

    rows = 256
    ng = ng_ref[...]

    def out_body(j, _):
        r = pl.ds(pl.multiple_of(j * rows, rows), rows)
        o = of_scr[r, :] + ob_scr[r, :]
        o = o * lax.rsqrt(jnp.mean(o * o, axis=-1, keepdims=True) + RMS_EPS) * ng
        o_ref[r, :] = (o * _silu(g_ref[r, :].astype(F32))).astype(BF16)
        return 0

    lax.fori_loop(0, q_ref.shape[0] // rows, out_body, 0)


def _hgrn(cparts, parts, lb, norm_g, n_heads):
    b, t, _ = parts.shape
    tc = cparts.shape[1]
    w = HG_EXPAND

    def col(part):
        return lambda bi, hi: (bi, 0, part * n_heads + hi)

    cspec = lambda part: pl.BlockSpec((None, tc, w), col(part))
    lspec = lambda part: pl.BlockSpec((None, t, w), col(part))
    return pl.pallas_call(
        _hgrn_kernel,
        grid=(b, n_heads),
        in_specs=[pl.BlockSpec((2, w), lambda bi, hi: (0, hi)),
                  pl.BlockSpec((1, w), lambda bi, hi: (0, hi)),
                  cspec(1), cspec(2), cspec(3),
                  lspec(0), lspec(1), lspec(2), lspec(3), lspec(4)],
        out_specs=pl.BlockSpec((None, t, w), lambda bi, hi: (bi, 0, hi)),
        out_shape=jax.ShapeDtypeStruct((b, t, n_heads * w), BF16),
        scratch_shapes=[pltpu.VMEM((t, w), F32), pltpu.VMEM((t, w), F32),
                        pltpu.VMEM((2 * HG_CHUNK, w), F32), pltpu.VMEM((2 * HG_CHUNK, w), F32)],
        compiler_params=_cparams(("parallel", "parallel")),
        name="hgrn2",
    )(lb, norm_g.reshape(1, -1), cparts, cparts, cparts, parts, parts, parts, parts, parts)


def _rope(x, cos, sin):
    lane = lax.broadcasted_iota(jnp.int32, x.shape, 1)
    swapped = jnp.where((lane % 32) < 16,
                        pltpu.roll(x, LANES - 16, axis=1),
                        pltpu.roll(x, 16, axis=1))
    return x * cos + swapped * sin


def _da_kernel(lam_ref, q_ref, kc_ref, kl_ref, vc_ref, vl_ref, cq_ref, sq_ref, ck_ref, sk_ref, g_ref,
               o_ref, k_scr, v_scr):
    tc = kc_ref.shape[0]
    t = kl_ref.shape[0]
    dv = DA_V_DIM

    @pl.when(pl.program_id(2) == 0)
    def _():
        lane = lax.broadcasted_iota(jnp.int32, (tc + t, LANES), 1)
        v_scr[:, dv:dv + LANES] = jnp.where(lane == 0, 1.0, 0.0).astype(BF16)
        k_scr[:, 0:tc] = kc_ref[...].astype(F32).T.astype(BF16)
        v_scr[0:tc, 0:dv] = vc_ref[...]
        v_scr[tc:tc + t, 0:dv] = vl_ref[...]
        rows = min(512, t)
        for j in range(t // rows):
            r = slice(j * rows, (j + 1) * rows)
            kr = _rope(kl_ref[r, :].astype(F32), ck_ref[r, :], sk_ref[r, :])
            k_scr[:, tc + j * rows:tc + (j + 1) * rows] = kr.T.astype(BF16)

    lam = lam_ref[0]
    out_scale = lam_ref[1]
    qscale = (DA_HEAD_DIM ** -0.5) * math.log2(math.e)
    k = k_scr[...]
    v1 = v_scr[...]

    def branch(qm):
        s = _nn(qm.astype(BF16), k)
        p = jnp.exp2(s - jnp.max(s, axis=-1, keepdims=True)).astype(BF16)
        ov = _nn(p, v1)
        return ov[:, 0:dv] / ov[:, dv:dv + 1]

    sub = min(DA_Q_SUB, q_ref.shape[0])
    for j in range(q_ref.shape[0] // sub):
        r = slice(j * sub, (j + 1) * sub)
        qr = _rope(q_ref[r, :].astype(F32), cq_ref[r, :], sq_ref[r, :]) * qscale
        lane = lax.broadcasted_iota(jnp.int32, qr.shape, 1)
        o = (branch(jnp.where(lane < DA_HEAD_DIM, qr, 0.0))
             - lam * branch(jnp.where(lane >= DA_HEAD_DIM, qr, 0.0)))
        o = o * lax.rsqrt(jnp.mean(o * o, axis=-1, keepdims=True) + RMS_EPS)
        o_ref[r, :] = (o * g_ref[...] * out_scale).astype(BF16)


def _diff_attn(cparts, parts, lam2, subln_g, cos_t, sin_t, n_heads, col0, tq):
    b, t, _ = parts.shape
    tc = cparts.shape[1]
    w = LANES

    def col(part):
        return lambda bi, hi, qi: (bi, 0, col0 + part * n_heads + hi)

    return pl.pallas_call(
        _da_kernel,
        grid=(b, n_heads, t // tq),
        in_specs=[pl.BlockSpec(memory_space=pltpu.SMEM),
                  pl.BlockSpec((None, tq, w), lambda bi, hi, qi: (bi, qi, col0 + hi)),
                  pl.BlockSpec((None, tc, w), col(1)),
                  pl.BlockSpec((None, t, w), col(1)),
                  pl.BlockSpec((None, tc, w), col(2)),
                  pl.BlockSpec((None, t, w), col(2)),
                  pl.BlockSpec((tq, w), lambda bi, hi, qi: (qi, 0)),
                  pl.BlockSpec((tq, w), lambda bi, hi, qi: (qi, 0)),
                  pl.BlockSpec((t, w), lambda bi, hi, qi: (0, 0)),
                  pl.BlockSpec((t, w), lambda bi, hi, qi: (0, 0)),
                  pl.BlockSpec((1, w), lambda bi, hi, qi: (0, hi))],
        out_specs=pl.BlockSpec((None, tq, w), lambda bi, hi, qi: (bi, qi, hi)),
        out_shape=jax.ShapeDtypeStruct((b, t, n_heads * w), BF16),
        scratch_shapes=[pltpu.VMEM((w, tc + t), BF16), pltpu.VMEM((tc + t, DA_V_DIM + w), BF16)],
        compiler_params=_cparams(("parallel", "parallel", "arbitrary")),
        name="diff_attn",
    )(lam2, parts, cparts, parts, cparts, parts, cos_t, sin_t, cos_t, sin_t, subln_g.reshape(1, -1))


def _outproj_kernel(hg_ref, da_ref, x_ref, wa_ref, wb_ref, g1_ref, sc_ref, sh_ref, ng_ref, x1_ref, h2_ref):
    mix = _nn(hg_ref[...], wa_ref[...]) + _nn(da_ref[...], wb_ref[...])
    x1 = x_ref[...] + g1_ref[0] * mix
    x1_ref[...] = x1
    y = x1 * lax.rsqrt(jnp.mean(x1 * x1, axis=-1, keepdims=True) + RMS_EPS) * ng_ref[...]
    h2_ref[...] = (y * (1.0 + sc_ref[0]) + sh_ref[0]).astype(BF16)


def _outproj(hg, da, x2, w_out, g1, sc2, sh2, norm_g, rows_per_mod, tm):
    m, d = x2.shape
    half = hg.shape[1]
    bpm = rows_per_mod // tm
    mod = pl.BlockSpec((1, 1, d), lambda i: (i // bpm, 0, 0))
    return pl.pallas_call(
        _outproj_kernel,
        grid=(m // tm,),
        in_specs=[pl.BlockSpec((tm, half), lambda i: (i, 0)),
                  pl.BlockSpec((tm, half), lambda i: (i, 0)),
                  pl.BlockSpec((tm, d), lambda i: (i, 0)),
                  pl.BlockSpec((half, d), lambda i: (0, 0)),
                  pl.BlockSpec((half, d), lambda i: (1, 0)),
                  mod, mod, mod,
                  pl.BlockSpec((1, d), lambda i: (0, 0))],
        out_specs=[pl.BlockSpec((tm, d), lambda i: (i, 0)),
                   pl.BlockSpec((tm, d), lambda i: (i, 0))],
        out_shape=[jax.ShapeDtypeStruct((m, d), F32), jax.ShapeDtypeStruct((m, d), BF16)],
        compiler_params=_cparams(("parallel",)),
        name="outproj",
    )(hg, da, x2, w_out, w_out, g1, sc2, sh2, norm_g)


def _top_values(s, n):
    vals = []
    rank = jnp.full(s.shape, float(n), F32)
    for r in range(n):
        m = jnp.max(s, axis=0, keepdims=True)
        hit = s == m
        vals.append(m)
        rank = jnp.where(hit, float(r), rank)
        s = jnp.where(hit, NEG_INF, s)
    return vals, rank


def _route_tables(s1, s2):
    kk = PEER_TOPK
    v1, rank1 = _top_values(s1, kk)
    v2, rank2 = _top_values(s2, kk)
    sv2 = jnp.concatenate(v2, axis=0)
    cands = [v1[0] + sv2]
    cands += [v1[r] + sv2[0:8, :] for r in range(1, 8)]
    cands.append(jnp.concatenate(v1[8:16], axis=0) + v2[0])
    top = v1[0] + v2[0]
    z = jnp.zeros_like(top)
    tau = top
    work = cands
    for _ in range(kk):
        m8 = work[1]
        for cnd in work[2:]:
            m8 = jnp.maximum(m8, cnd)
        tau = jnp.maximum(jnp.max(work[0], axis=0, keepdims=True), jnp.max(m8, axis=0, keepdims=True))
        z = z + jnp.exp(tau - top)
        work = [jnp.where(cnd == tau, NEG_INF, cnd) for cnd in work]
    cnt = [jnp.sum(jnp.where(cands[r] >= tau, 1.0, 0.0), axis=0, keepdims=True) for r in range(8)]
    tail = jnp.where(cands[8] >= tau, 1.0, 0.0)
    cnt += [tail[r:r + 1, :] for r in range(8)]
    n1 = jnp.zeros_like(s1)
    for r in range(kk):
        n1 = jnp.where(rank1 == float(r), cnt[r], n1)
    return n1, jnp.exp(s1 - v1[0]), rank2.astype(BF16), (jnp.exp(s2 - v2[0]) / z).astype(BF16)


def _route_kernel(h_ref, wq_ref, sk_ref, n1_ref, a_ref, r2_ref, b_ref):
    tp = h_ref.shape[0]
    q = _nn(h_ref[...], wq_ref[...]).astype(BF16)
    for h in range(PEER_HEADS):
        s1 = _nt(sk_ref[2 * h], q[:, (2 * h) * LANES:(2 * h + 1) * LANES])
        s2 = _nt(sk_ref[2 * h + 1], q[:, (2 * h + 1) * LANES:(2 * h + 2) * LANES])
        for c in range(tp // LANES):
            lanes = slice(c * LANES, (c + 1) * LANES)
            n1, a, r2, b = _route_tables(s1[:, lanes], s2[:, lanes])
            n1_ref[h, :, lanes] = n1
            a_ref[h, :, lanes] = a
            r2_ref[h, :, lanes] = r2
            b_ref[h, :, lanes] = b


def _route(h2, wq, sk, tp):
    p, d = h2.shape
    nq = wq.shape[1]
    tab = lambda: pl.BlockSpec((PEER_HEADS, PEER_NKEYS, tp), lambda i: (0, 0, i))
    return pl.pallas_call(
        _route_kernel,
        grid=(p // tp,),
        in_specs=[pl.BlockSpec((tp, d), lambda i: (i, 0)),
                  pl.BlockSpec((d, nq), lambda i: (0, 0)),
                  pl.BlockSpec((2 * PEER_HEADS, PEER_NKEYS, LANES), lambda i: (0, 0, 0))],
        out_specs=[tab(), tab(), tab(), tab()],
        out_shape=[jax.ShapeDtypeStruct((PEER_HEADS, PEER_NKEYS, p), F32),
                   jax.ShapeDtypeStruct((PEER_HEADS, PEER_NKEYS, p), F32),
                   jax.ShapeDtypeStruct((PEER_HEADS, PEER_NKEYS, p), BF16),
                   jax.ShapeDtypeStruct((PEER_HEADS, PEER_NKEYS, p), BF16)],
        compiler_params=_cparams(("parallel",)),
        name="peer_route",
    )(h2, wq, sk)


PEER_SUB = 512
PEER_LANE_CHUNK = 256


def _peer_kernel(x1_ref, g2_ref, fg_ref, h_ref, n1_ref, a_ref, r2_ref, b_ref, u_ref, v_ref, o_ref,
                 acc, ht_scr, act_a, act_b):
    e = pl.program_id(1)
    te, tp = ht_scr.shape
    n_slab = te // PEER_NKEYS
    slabs_per_sub = PEER_SUB // PEER_NKEYS
    zero = jnp.zeros((), BF16)

    @pl.when(e == 0)
    def _():
        acc[...] = jnp.zeros_like(acc)

    n_sub = te // PEER_SUB
    act_bufs = (act_a, act_b)

    def first_matmul(sub):
        act_bufs[sub % 2][...] = _nt(u_ref[sub * PEER_SUB:(sub + 1) * PEER_SUB, :], h_ref[...])

    first_matmul(0)
    for sub in range(n_sub):
        if sub + 1 < n_sub:
            first_matmul(sub + 1)
        act = act_bufs[sub % 2]
        for s in range(slabs_per_sub):
            sl = sub * slabs_per_sub + s
            i1 = e * n_slab + sl
            n1rows = [n1_ref[h, pl.ds(i1, 1), :] for h in range(PEER_HEADS)]
            arows = [a_ref[h, pl.ds(i1, 1), :] for h in range(PEER_HEADS)]
            for lc in range(tp // PEER_LANE_CHUNK):
                lanes = slice(lc * PEER_LANE_CHUNK, (lc + 1) * PEER_LANE_CHUNK)
                tile = (BF16_SUBLANES, PEER_LANE_CHUNK)
                n1b = [jnp.broadcast_to(r[:, lanes], tile).astype(BF16) for r in n1rows]
                ab = [jnp.broadcast_to(r[:, lanes], tile).astype(BF16) for r in arows]
                for g in range(PEER_NKEYS // BF16_SUBLANES):
                    keys = slice(g * BF16_SUBLANES, (g + 1) * BF16_SUBLANES)
                    gate = None
                    for h in range(PEER_HEADS):
                        term = jnp.where(r2_ref[h, keys, lanes] < n1b[h], b_ref[h, keys, lanes], zero) * ab[h]
                        gate = term if gate is None else gate + term
                    rows = slice(s * PEER_NKEYS + g * BF16_SUBLANES, s * PEER_NKEYS + (g + 1) * BF16_SUBLANES)
                    a = act[rows, lanes].astype(BF16)
                    gelu = 0.5 * a * (1.0 + lax.erf(a * math.sqrt(0.5)))
                    rows = slice(sl * PEER_NKEYS + g * BF16_SUBLANES, sl * PEER_NKEYS + (g + 1) * BF16_SUBLANES)
                    ht_scr[rows, lanes] = gate * gelu
    acc[...] += _tn(ht_scr[...], v_ref[...])

    @pl.when(e == pl.num_programs(1) - 1)
    def _():
        x = x1_ref[...] + g2_ref[0] * acc[...]
        o_ref[...] = x * lax.rsqrt(jnp.mean(x * x, axis=-1, keepdims=True) + RMS_EPS) * fg_ref[...]


def _peer(x1, g2, fg, h2, n1, a, r2, bb, u_bf, v_bf, rows_per_mod, tp, te):
    p, d = x1.shape
    n_exp = u_bf.shape[0]
    bpm = rows_per_mod // tp
    tab = lambda: pl.BlockSpec((PEER_HEADS, PEER_NKEYS, tp), lambda i, e: (0, 0, i))
    return pl.pallas_call(
        _peer_kernel,
        grid=(p // tp, n_exp // te),
        in_specs=[pl.BlockSpec((tp, d), lambda i, e: (i, 0), pipeline_mode=pl.Buffered(1)),
                  pl.BlockSpec((1, 1, d), lambda i, e: (i // bpm, 0, 0)),
                  pl.BlockSpec((1, d), lambda i, e: (0, 0)),
                  pl.BlockSpec((tp, d), lambda i, e: (i, 0), pipeline_mode=pl.Buffered(1)),
                  tab(), tab(), tab(), tab(),
                  pl.BlockSpec((te, d), lambda i, e: (e, 0)),
                  pl.BlockSpec((te, d), lambda i, e: (e, 0))],
        out_specs=pl.BlockSpec((tp, d), lambda i, e: (i, 0)),
        out_shape=jax.ShapeDtypeStruct((p, d), F32),
        scratch_shapes=[pltpu.VMEM((tp, d), F32),
                        pltpu.VMEM((te, tp), BF16),
                        pltpu.VMEM((PEER_SUB, tp), F32),
                        pltpu.VMEM((PEER_SUB, tp), F32)],
        compiler_params=_cparams(("parallel", "arbitrary")),
        name="peer_experts",
    )(x1, g2, fg, h2, n1, a, r2, bb, u_bf, v_bf)


class _Tiles(NamedTuple):
    adaln_cols: int
    inproj_rows: int
    inproj_ctx_rows: int
    inproj_cols: int
    attn_queries: int
    outproj_rows: int
    route_tokens: int
    peer_tokens: int
    peer_experts: int


def _tiles(t, ctx_rows):
    return _Tiles(adaln_cols=1024, inproj_rows=min(1024, t), inproj_ctx_rows=min(512, ctx_rows),
                  inproj_cols=1024, attn_queries=min(2048, t), outproj_rows=min(512, t),
                  route_tokens=min(256, t), peer_tokens=min(512, t), peer_experts=1024)


def _rope_tables(t):
    rows = t // GRID_W
    row = jnp.repeat(jnp.arange(rows, dtype=F32), GRID_W)
    colp = jnp.tile(jnp.arange(GRID_W, dtype=F32), rows)
    n_freq = DA_HEAD_DIM // 4
    inv_freq = ROPE_BASE ** (-jnp.arange(n_freq, dtype=F32) / n_freq)
    ar = row[:, None] * inv_freq[None, :]
    ac = colp[:, None] * inv_freq[None, :]
    cos64 = jnp.concatenate([jnp.cos(ar), jnp.cos(ar), jnp.cos(ac), jnp.cos(ac)], axis=-1)
    sin64 = jnp.concatenate([-jnp.sin(ar), jnp.sin(ar), -jnp.sin(ac), jnp.sin(ac)], axis=-1)
    return jnp.tile(cos64, (1, 2)), jnp.tile(sin64, (1, 2))


def _layer(x, c, ctx, c_ctx, w_ada, b_ada, norm1_g, w_in, lb, hg_norm_g, lam, lam_init, da_subln_g,
           w_out, norm2_g, peer_w_query, peer_sub_keys, peer_u, peer_v, final_norm_g):
    b, t, d = x.shape
    tc = ctx.shape[1]
    hg_width = hg_norm_g.shape[0]
    hg_heads = hg_width // HG_EXPAND
    da_heads = da_subln_g.shape[0] // DA_V_DIM
    tiles = _tiles(t, b * tc)

    cond = jnp.zeros((8, d), F32).at[0:b].set(c).at[b].set(c_ctx)
    mod = _adaln(cond, w_ada, b_ada, tiles.adaln_cols)
    sh1, sc1, g1, sh2, sc2, g2 = [m[0:b].reshape(b, 1, d) for m in jnp.split(mod, 6, axis=-1)]
    csh1, csc1 = [m[b:b + 1].reshape(1, 1, d) for m in jnp.split(mod, 6, axis=-1)[0:2]]

    w_in_bf = w_in.astype(BF16)
    n_cols = w_in.shape[1]
    g1n = norm1_g.reshape(1, d)
    parts = _inproj(x.reshape(b * t, d), sc1, sh1, g1n, w_in_bf, t,
                    tiles.inproj_rows, tiles.inproj_cols).reshape(b, t, n_cols)
    cparts = _inproj(ctx.reshape(b * tc, d), csc1, csh1, g1n, w_in_bf, b * tc,
                     tiles.inproj_ctx_rows, tiles.inproj_cols).reshape(b, tc, n_cols)

    hg = _hgrn(cparts, parts, lb, hg_norm_g, hg_heads)
    cos_t, sin_t = _rope_tables(t)
    lam2 = jnp.stack([lam, jnp.asarray(1.0 - lam_init, F32)]).astype(F32)
    da = _diff_attn(cparts, parts, lam2, da_subln_g, cos_t, sin_t, da_heads, 5 * hg_heads, tiles.attn_queries)

    x2 = x.reshape(b * t, d)
    x1, h2 = _outproj(hg.reshape(b * t, -1), da.reshape(b * t, -1), x2, w_out.astype(BF16),
                      g1, sc2, sh2, norm2_g.reshape(1, d), t, tiles.outproj_rows)

    sk = peer_sub_keys.reshape(2 * PEER_HEADS, PEER_NKEYS, -1).astype(BF16)
    n1, a, r2, bb = _route(h2, peer_w_query.astype(BF16), sk, tiles.route_tokens)
    out = _peer(x1, g2, final_norm_g.reshape(1, d), h2, n1, a, r2, bb,
                peer_u.astype(BF16), peer_v.astype(BF16), t, tiles.peer_tokens, tiles.peer_experts)
    return out.reshape(b, t, d)


def kernel(x, c, ctx, c_ctx, w_ada, b_ada, norm1_g, w_in, hg_gamma, hg_norm_g, da_lambda_q1, da_lambda_k1,
           da_lambda_q2, da_lambda_k2, da_subln_g, w_out, norm2_g, peer_w_query, peer_sub_keys, peer_u,
           peer_v, final_norm_g):
    depth = w_ada.shape[0]
    assert depth == 1, "single-layer block: the context stream is never updated"
    lb_all = jnp.cumsum(jax.nn.softmax(hg_gamma.astype(F32), axis=0), axis=0)
    l = 0
    lam_init = 0.8 - 0.6 * math.exp(-0.3 * l)
    lam = (jnp.exp(jnp.sum(da_lambda_q1[l].astype(F32) * da_lambda_k1[l].astype(F32)))
           - jnp.exp(jnp.sum(da_lambda_q2[l].astype(F32) * da_lambda_k2[l].astype(F32))) + lam_init)
    return _layer(x, c, ctx, c_ctx, w_ada[l], b_ada[l], norm1_g[l], w_in[l], lb_all[l], hg_norm_g[l],
                  lam, lam_init, da_subln_g[l], w_out[l], norm2_g[l], peer_w_query[l], peer_sub_keys[l],
                  peer_u[l], peer_v[l], final_norm_g)
```

```python
import math
from typing import NamedTuple

import jax
import jax.numpy as jnp
from jax import lax
from jax.experimental import pallas as pl
from jax.experimental.pallas import tpu as pltpu

F32 = jnp.float32
BF16 = jnp.bfloat16

GRID_W = 64
HG_EXPAND = 128
HG_CHUNK = 64
HG_UNROLL = 4
DA_HEAD_DIM = 64
DA_V_DIM = 128
DA_Q_SUB = 256
ROPE_BASE = 10000.0
PEER_HEADS = 8
PEER_NKEYS = 128
PEER_TOPK = 16
RMS_EPS = 1e-6
LANES = 128
BF16_SUBLANES = 16
EXP_CLAMP = 80.0
V7X_VMEM_BYTES = 64 * 1024 * 1024
VMEM_LIMIT = V7X_VMEM_BYTES - 8 * 1024 * 1024
NEG_INF = float("-inf")


def _nt(a, b):
    return lax.dot_general(a, b, (((1,), (1,)), ((), ())), preferred_element_type=F32)


def _tn(a, b):
    return lax.dot_general(a, b, (((0,), (0,)), ((), ())), preferred_element_type=F32)


def _nn(a, b):
    return jnp.dot(a, b, preferred_element_type=F32)


def _cparams(sem):
    return pltpu.CompilerParams(dimension_semantics=sem, vmem_limit_bytes=VMEM_LIMIT)


def _adaln_kernel(c_ref, w_ref, b_ref, o_ref):
    c = c_ref[...]
    a = (c * jax.nn.sigmoid(c)).astype(BF16)
    o_ref[...] = _nn(a, w_ref[...].astype(BF16)) + b_ref[...]


def _adaln(cond, w, b, tn):
    d, n = w.shape
    return pl.pallas_call(
        _adaln_kernel,
        grid=(n // tn,),
        in_specs=[pl.BlockSpec((8, d), lambda j: (0, 0)),
                  pl.BlockSpec((d, tn), lambda j: (0, j)),
                  pl.BlockSpec((1, tn), lambda j: (0, j))],
        out_specs=pl.BlockSpec((8, tn), lambda j: (0, j)),
        out_shape=jax.ShapeDtypeStruct((8, n), F32),
        compiler_params=_cparams(("arbitrary",)),
        name="adaln",
    )(cond, w, b.reshape(1, n))


def _inproj_kernel(x_ref, sc_ref, sh_ref, g_ref, w_ref, o_ref, h_scr):
    @pl.when(pl.program_id(1) == 0)
    def _():
        x = x_ref[...]
        y = x * lax.rsqrt(jnp.mean(x * x, axis=-1, keepdims=True) + RMS_EPS) * g_ref[...]
        h_scr[...] = (y * (1.0 + sc_ref[0]) + sh_ref[0]).astype(BF16)

    o_ref[...] = _nn(h_scr[...], w_ref[...]).astype(BF16)


def _inproj(x2, sc, sh, g, w, rows_per_mod, tm, tn):
    m, d = x2.shape
    n = w.shape[1]
    bpm = rows_per_mod // tm
    return pl.pallas_call(
        _inproj_kernel,
        grid=(m // tm, n // tn),
        in_specs=[pl.BlockSpec((tm, d), lambda i, j: (i, 0)),
                  pl.BlockSpec((1, 1, d), lambda i, j: (i // bpm, 0, 0)),
                  pl.BlockSpec((1, 1, d), lambda i, j: (i // bpm, 0, 0)),
                  pl.BlockSpec((1, d), lambda i, j: (0, 0)),
                  pl.BlockSpec((d, tn), lambda i, j: (0, j))],
        out_specs=pl.BlockSpec((tm, tn), lambda i, j: (i, j)),
        out_shape=jax.ShapeDtypeStruct((m, n), BF16),
        scratch_shapes=[pltpu.VMEM((tm, d), BF16)],
        compiler_params=_cparams(("parallel", "arbitrary")),
        name="inproj",
    )(x2, sc, sh, g, w)


def _hgrn_gates(z, lb, tri):
    f = lb + (1.0 - lb) * jax.nn.sigmoid(z)
    lf = jnp.log(f)
    hi = lf.astype(BF16)
    lo = (lf - hi.astype(F32)).astype(BF16)
    both = _nn(tri, jnp.concatenate([hi, lo], axis=1))
    w = lf.shape[1]
    return 1.0 - f, both[:, 0:w] + both[:, w:2 * w]


def _hgrn_masks(rev):
    n = 2 * HG_CHUNK
    row = lax.broadcasted_iota(jnp.int32, (n, n), 0)
    col = lax.broadcasted_iota(jnp.int32, (n, n), 1)
    same = (row >= HG_CHUNK) == (col >= HG_CHUNK)
    before = (col >= row) if rev else (row >= col)
    diag = same & before
    cross = (~same) & before
    return jnp.where(diag, 1.0, 0.0).astype(BF16), diag, cross


def _hgrn_block(st, z, v_bf, q, lb, masks, rev):
    c = HG_CHUNK
    tri, diag, cross = masks
    k, cum = _hgrn_gates(z, lb, tri)
    second = lax.broadcasted_iota(jnp.int32, cum.shape, 0) < c if rev else \
        lax.broadcasted_iota(jnp.int32, cum.shape, 0) >= c
    tot_lo = cum[0:1, :] if rev else cum[c - 1:c, :]
    tot_hi = cum[c:c + 1, :] if rev else cum[2 * c - 1:2 * c, :]
    tot_first, tot_second = (tot_hi, tot_lo) if rev else (tot_lo, tot_hi)
    tot_own = jnp.where(lax.broadcasted_iota(jnp.int32, cum.shape, 0) < c, tot_lo, tot_hi)
    k_st = k * jnp.exp(tot_own - cum)
    k_end = k_st * jnp.where(second, 1.0, jnp.exp(tot_second))
    st_new = st * jnp.exp(tot_first + tot_second) + _tn(v_bf, k_end.astype(BF16))
    if q is None:
        return st_new, None, None, None
    mid_lo = cum[c // 2:c // 2 + 1, :] if rev else cum[c // 2 - 1:c // 2, :]
    mid_hi = cum[c + c // 2:c + c // 2 + 1, :] if rev else cum[c + c // 2 - 1:c + c // 2, :]
    rel = cum - jnp.where(lax.broadcasted_iota(jnp.int32, cum.shape, 0) < c, mid_lo, mid_hi)
    qt = q * jnp.exp(jnp.minimum(rel, EXP_CLAMP))
    kt = k * jnp.exp(jnp.minimum(-rel, EXP_CLAMP))
    q_in = q * jnp.exp(cum)
    scores = (jnp.where(diag, _nt(qt.astype(BF16), kt.astype(BF16)), 0.0)
              + jnp.where(cross, _nt(q_in.astype(BF16), k_st.astype(BF16)), 0.0))
    intra = _nn(scores.astype(BF16), v_bf)
    q_start = q_in * jnp.where(second, jnp.exp(tot_first), 1.0)
    inter = _nt(q_start.astype(BF16), st.astype(BF16))
    return st_new, intra, inter, jnp.abs(rel)


def _hgrn_intra_exact(z, v_bf, q, lb, masks, rev, cum_scr, k_scr):
    c = HG_CHUNK
    n = 2 * c
    tri, diag, cross = masks
    k, cum = _hgrn_gates(z, lb, tri)
    rows = lax.broadcasted_iota(jnp.int32, cum.shape, 0)
    if rev:
        cum = cum + jnp.where(rows < c, cum[c:c + 1, :], 0.0)
    else:
        cum = cum + jnp.where(rows >= c, cum[c - 1:c, :], 0.0)
    col = lax.broadcasted_iota(jnp.int32, (n, n), 1)
    cum_scr[...] = cum
    k_scr[...] = k

    def key_column(s, a):
        d = jnp.exp(jnp.minimum(cum - cum_scr[pl.ds(s, 1), :], 0.0))
        return jnp.where(col == s, jnp.sum(q * d * k_scr[pl.ds(s, 1), :], axis=-1, keepdims=True), a)

    a = lax.fori_loop(0, n, key_column, jnp.zeros((n, n), F32))
    return _nn(jnp.where(diag | cross, a, 0.0).astype(BF16), v_bf)


def _silu(x):
    return x * jax.nn.sigmoid(x)


def _hgrn_kernel(lb_ref, ng_ref, czf_ref, czb_ref, ci_ref, q_ref, zf_ref, zb_ref, i_ref, g_ref,
                 o_ref, of_scr, ob_scr, cum_scr, k_scr):
    n = 2 * HG_CHUNK
    n_ctx = czf_ref.shape[0] // n
    n_lat = q_ref.shape[0] // n
    un = math.gcd(HG_UNROLL, n_lat)
    masks_f = _hgrn_masks(False)
    masks_b = _hgrn_masks(True)
    lb_f = lb_ref[0:1, :]
    lb_b = lb_ref[1:2, :]

    def rows_of(block):
        return pl.ds(pl.multiple_of(block * n, n), n)

    def ctx_body(j, carry):
        sf, sb = carry
        rf = rows_of(j)
        rb = rows_of(n_ctx - 1 - j)
        sf = _hgrn_block(sf, czf_ref[rf, :].astype(F32), ci_ref[rf, :], None, lb_f, masks_f, False)[0]
        sb = _hgrn_block(sb, czb_ref[rb, :].astype(F32), ci_ref[rb, :], None, lb_b, masks_b, True)[0]
        return sf, sb

    s0 = jnp.zeros((HG_EXPAND, HG_EXPAND), F32)
    sf, sb = lax.fori_loop(0, n_ctx, ctx_body, (s0, s0))

    def lat_body(j, carry):
        sf, sb = carry
        worst = None
        redo = []
        for u in range(un):
            rf = rows_of(j * un + u)
            rb = rows_of(n_lat - 1 - (j * un + u))
            sf, intra_f, inter_f, ex_f = _hgrn_block(sf, zf_ref[rf, :].astype(F32), i_ref[rf, :],
                                                     _silu(q_ref[rf, :].astype(F32)), lb_f, masks_f, False)
            sb, intra_b, inter_b, ex_b = _hgrn_block(sb, zb_ref[rb, :].astype(F32), i_ref[rb, :],
                                                     _silu(q_ref[rb, :].astype(F32)), lb_b, masks_b, True)
            of_scr[rf, :] = intra_f + inter_f
            ob_scr[rb, :] = intra_b + inter_b
            ex = jnp.maximum(ex_f, ex_b)
            worst = ex if worst is None else jnp.maximum(worst, ex)
            redo.append((rf, rb, inter_f, inter_b))

        @pl.when(jnp.max(worst) > EXP_CLAMP)
        def _():
            for rf, rb, inter_f, inter_b in redo:
                of_scr[rf, :] = inter_f + _hgrn_intra_exact(
                    zf_ref[rf, :].astype(F32), i_ref[rf, :], _silu(q_ref[rf, :].astype(F32)), lb_f, masks_f, False,
                    cum_scr, k_scr)
                ob_scr[rb, :] = inter_b + _hgrn_intra_exact(
                    zb_ref[rb, :].astype(F32), i_ref[rb, :], _silu(q_ref[rb, :].astype(F32)), lb_b, masks_b, True,
                    cum_scr, k_scr)

        return sf, sb

    lax.fori_loop(0, n_lat // un, lat_body, (sf, sb))

    rows = 256
    ng = ng_ref[...]

    def out_body(j, _):
        r = pl.ds(pl.multiple_of(j * rows, rows), rows)
        o = of_scr[r, :] + ob_scr[r, :]
        o = o * lax.rsqrt(jnp.mean(o * o, axis=-1, keepdims=True) + RMS_EPS) * ng
        o_ref[r, :] = (o * _silu(g_ref[r, :].astype(F32))).astype(BF16)
        return 0

    lax.fori_loop(0, q_ref.shape[0] // rows, out_body, 0)


def _hgrn(cparts, parts, lb, norm_g, n_heads):
    b, t, _ = parts.shape
    tc = cparts.shape[1]
    w = HG_EXPAND

    def col(part):
        return lambda bi, hi: (bi, 0, part * n_heads + hi)

    cspec = lambda part: pl.BlockSpec((None, tc, w), col(part))
    lspec = lambda part: pl.BlockSpec((None, t, w), col(part))
    return pl.pallas_call(
        _hgrn_kernel,
        grid=(b, n_heads),
        in_specs=[pl.BlockSpec((2, w), lambda bi, hi: (0, hi)),
                  pl.BlockSpec((1, w), lambda bi, hi: (0, hi)),
                  cspec(1), cspec(2), cspec(3),
                  lspec(0), lspec(1), lspec(2), lspec(3), lspec(4)],
        out_specs=pl.BlockSpec((None, t, w), lambda bi, hi: (bi, 0, hi)),
        out_shape=jax.ShapeDtypeStruct((b, t, n_heads * w), BF16),
        scratch_shapes=[pltpu.VMEM((t, w), F32), pltpu.VMEM((t, w), F32),
                        pltpu.VMEM((2 * HG_CHUNK, w), F32), pltpu.VMEM((2 * HG_CHUNK, w), F32)],
        compiler_params=_cparams(("parallel", "parallel")),
        name="hgrn2",
    )(lb, norm_g.reshape(1, -1), cparts, cparts, cparts, parts, parts, parts, parts, parts)


def _rope(x, cos, sin):
    lane = lax.broadcasted_iota(jnp.int32, x.shape, 1)
    swapped = jnp.where((lane % 32) < 16,
                        pltpu.roll(x, LANES - 16, axis=1),
                        pltpu.roll(x, 16, axis=1))
    return x * cos + swapped * sin


def _da_kernel(lam_ref, q_ref, kc_ref, kl_ref, vc_ref, vl_ref, cq_ref, sq_ref, ck_ref, sk_ref, g_ref,
               o_ref, k_scr, v_scr):
    tc = kc_ref.shape[0]
    t = kl_ref.shape[0]
    dv = DA_V_DIM

    @pl.when(pl.program_id(2) == 0)
    def _():
        lane = lax.broadcasted_iota(jnp.int32, (tc + t, LANES), 1)
        v_scr[:, dv:dv + LANES] = jnp.where(lane == 0, 1.0, 0.0).astype(BF16)
        k_scr[:, 0:tc] = kc_ref[...].astype(F32).T.astype(BF16)
        v_scr[0:tc, 0:dv] = vc_ref[...]
        v_scr[tc:tc + t, 0:dv] = vl_ref[...]
        rows = min(512, t)
        for j in range(t // rows):
            r = slice(j * rows, (j + 1) * rows)
            kr = _rope(kl_ref[r, :].astype(F32), ck_ref[r, :], sk_ref[r, :])
            k_scr[:, tc + j * rows:tc + (j + 1) * rows] = kr.T.astype(BF16)

    lam = lam_ref[0]
    out_scale = lam_ref[1]
    qscale = (DA_HEAD_DIM ** -0.5) * math.log2(math.e)
    k = k_scr[...]
    v1 = v_scr[...]

    def branch(qm):
        s = _nn(qm.astype(BF16), k)
        p = jnp.exp2(s - jnp.max(s, axis=-1, keepdims=True)).astype(BF16)
        ov = _nn(p, v1)
        return ov[:, 0:dv] / ov[:, dv:dv + 1]

    sub = min(DA_Q_SUB, q_ref.shape[0])
    for j in range(q_ref.shape[0] // sub):
        r = slice(j * sub, (j + 1) * sub)
        qr = _rope(q_ref[r, :].astype(F32), cq_ref[r, :], sq_ref[r, :]) * qscale
        lane = lax.broadcasted_iota(jnp.int32, qr.shape, 1)
        o = (branch(jnp.where(lane < DA_HEAD_DIM, qr, 0.0))
             - lam * branch(jnp.where(lane >= DA_HEAD_DIM, qr, 0.0)))
        o = o * lax.rsqrt(jnp.mean(o * o, axis=-1, keepdims=True) + RMS_EPS)
        o_ref[r, :] = (o * g_ref[...] * out_scale).astype(BF16)


def _diff_attn(cparts, parts, lam2, subln_g, cos_t, sin_t, n_heads, col0, tq):
    b, t, _ = parts.shape
    tc = cparts.shape[1]
    w = LANES

    def col(part):
        return lambda bi, hi, qi: (bi, 0, col0 + part * n_heads + hi)

    return pl.pallas_call(
        _da_kernel,
        grid=(b, n_heads, t // tq),
        in_specs=[pl.BlockSpec(memory_space=pltpu.SMEM),
                  pl.BlockSpec((None, tq, w), lambda bi, hi, qi: (bi, qi, col0 + hi)),
                  pl.BlockSpec((None, tc, w), col(1)),
                  pl.BlockSpec((None, t, w), col(1)),
                  pl.BlockSpec((None, tc, w), col(2)),
                  pl.BlockSpec((None, t, w), col(2)),
                  pl.BlockSpec((tq, w), lambda bi, hi, qi: (qi, 0)),
                  pl.BlockSpec((tq, w), lambda bi, hi, qi: (qi, 0)),
                  pl.BlockSpec((t, w), lambda bi, hi, qi: (0, 0)),
                  pl.BlockSpec((t, w), lambda bi, hi, qi: (0, 0)),
                  pl.BlockSpec((1, w), lambda bi, hi, qi: (0, hi))],
        out_specs=pl.BlockSpec((None, tq, w), lambda bi, hi, qi: (bi, qi, hi)),
        out_shape=jax.ShapeDtypeStruct((b, t, n_heads * w), BF16),
        scratch_shapes=[pltpu.VMEM((w, tc + t), BF16), pltpu.VMEM((tc + t, DA_V_DIM + w), BF16)],
        compiler_params=_cparams(("parallel", "parallel", "arbitrary")),
        name="diff_attn",
    )(lam2, parts, cparts, parts, cparts, parts, cos_t, sin_t, cos_t, sin_t, subln_g.reshape(1, -1))


def _outproj_kernel(hg_ref, da_ref, x_ref, wa_ref, wb_ref, g1_ref, sc_ref, sh_ref, ng_ref, x1_ref, h2_ref):
    mix = _nn(hg_ref[...], wa_ref[...]) + _nn(da_ref[...], wb_ref[...])
    x1 = x_ref[...] + g1_ref[0] * mix
    x1_ref[...] = x1
    y = x1 * lax.rsqrt(jnp.mean(x1 * x1, axis=-1, keepdims=True) + RMS_EPS) * ng_ref[...]
    h2_ref[...] = (y * (1.0 + sc_ref[0]) + sh_ref[0]).astype(BF16)


def _outproj(hg, da, x2, w_out, g1, sc2, sh2, norm_g, rows_per_mod, tm):
    m, d = x2.shape
    half = hg.shape[1]
    bpm = rows_per_mod // tm
    mod = pl.BlockSpec((1, 1, d), lambda i: (i // bpm, 0, 0))
    return pl.pallas_call(
        _outproj_kernel,
        grid=(m // tm,),
        in_specs=[pl.BlockSpec((tm, half), lambda i: (i, 0)),
                  pl.BlockSpec((tm, half), lambda i: (i, 0)),
                  pl.BlockSpec((tm, d), lambda i: (i, 0)),
                  pl.BlockSpec((half, d), lambda i: (0, 0)),
                  pl.BlockSpec((half, d), lambda i: (1, 0)),
                  mod, mod, mod,
                  pl.BlockSpec((1, d), lambda i: (0, 0))],
        out_specs=[pl.BlockSpec((tm, d), lambda i: (i, 0)),
                   pl.BlockSpec((tm, d), lambda i: (i, 0))],
        out_shape=[jax.ShapeDtypeStruct((m, d), F32), jax.ShapeDtypeStruct((m, d), BF16)],
        compiler_params=_cparams(("parallel",)),
        name="outproj",
    )(hg, da, x2, w_out, w_out, g1, sc2, sh2, norm_g)


def _top_values(s, n):
    vals = []
    rank = jnp.full(s.shape, float(n), F32)
    for r in range(n):
        m = jnp.max(s, axis=0, keepdims=True)
        hit = s == m
        vals.append(m)
        rank = jnp.where(hit, float(r), rank)
        s = jnp.where(hit, NEG_INF, s)
    return vals, rank


def _route_tables(s1, s2):
    kk = PEER_TOPK
    v1, rank1 = _top_values(s1, kk)
    v2, rank2 = _top_values(s2, kk)
    sv2 = jnp.concatenate(v2, axis=0)
    cands = [v1[0] + sv2]
    cands += [v1[r] + sv2[0:8, :] for r in range(1, 8)]
    cands.append(jnp.concatenate(v1[8:16], axis=0) + v2[0])
    top = v1[0] + v2[0]
    z = jnp.zeros_like(top)
    tau = top
    work = cands
    for _ in range(kk):
        m8 = work[1]
        for cnd in work[2:]:
            m8 = jnp.maximum(m8, cnd)
        tau = jnp.maximum(jnp.max(work[0], axis=0, keepdims=True), jnp.max(m8, axis=0, keepdims=True))
        z = z + jnp.exp(tau - top)
        work = [jnp.where(cnd == tau, NEG_INF, cnd) for cnd in work]
    cnt = [jnp.sum(jnp.where(cands[r] >= tau, 1.0, 0.0), axis=0, keepdims=True) for r in range(8)]
    tail = jnp.where(cands[8] >= tau, 1.0, 0.0)
    cnt += [tail[r:r + 1, :] for r in range(8)]
    n1 = jnp.zeros_like(s1)
    for r in range(kk):
        n1 = jnp.where(rank1 == float(r), cnt[r], n1)
    return n1, jnp.exp(s1 - v1[0]), rank2.astype(BF16), (jnp.exp(s2 - v2[0]) / z).astype(BF16)


def _route_kernel(h_ref, wq_ref, sk_ref, n1_ref, a_ref, r2_ref, b_ref):
    tp = h_ref.shape[0]
    q = _nn(h_ref[...], wq_ref[...]).astype(BF16)
    for h in range(PEER_HEADS):
        s1 = _nt(sk_ref[2 * h], q[:, (2 * h) * LANES:(2 * h + 1) * LANES])
        s2 = _nt(sk_ref[2 * h + 1], q[:, (2 * h + 1) * LANES:(2 * h + 2) * LANES])
        for c in range(tp // LANES):
            lanes = slice(c * LANES, (c + 1) * LANES)
            n1, a, r2, b = _route_tables(s1[:, lanes], s2[:, lanes])
            n1_ref[h, :, lanes] = n1
            a_ref[h, :, lanes] = a
            r2_ref[h, :, lanes] = r2
            b_ref[h, :, lanes] = b


def _route(h2, wq, sk, tp):
    p, d = h2.shape
    nq = wq.shape[1]
    tab = lambda: pl.BlockSpec((PEER_HEADS, PEER_NKEYS, tp), lambda i: (0, 0, i))
    return pl.pallas_call(
        _route_kernel,
        grid=(p // tp,),
        in_specs=[pl.BlockSpec((tp, d), lambda i: (i, 0)),
                  pl.BlockSpec((d, nq), lambda i: (0, 0)),
                  pl.BlockSpec((2 * PEER_HEADS, PEER_NKEYS, LANES), lambda i: (0, 0, 0))],
        out_specs=[tab(), tab(), tab(), tab()],
        out_shape=[jax.ShapeDtypeStruct((PEER_HEADS, PEER_NKEYS, p), F32),
                   jax.ShapeDtypeStruct((PEER_HEADS, PEER_NKEYS, p), F32),
                   jax.ShapeDtypeStruct((PEER_HEADS, PEER_NKEYS, p), BF16),
                   jax.ShapeDtypeStruct((PEER_HEADS, PEER_NKEYS, p), BF16)],
        compiler_params=_cparams(("parallel",)),
        name="peer_route",
    )(h2, wq, sk)


PEER_SUB = 512
PEER_LANE_CHUNK = 256


def _peer_kernel(x1_ref, g2_ref, fg_ref, h_ref, n1_ref, a_ref, r2_ref, b_ref, u_ref, v_ref, o_ref,
                 acc, ht_scr, act_a, act_b):
    e = pl.program_id(1)
    te, tp = ht_scr.shape
    n_slab = te // PEER_NKEYS
    slabs_per_sub = PEER_SUB // PEER_NKEYS
    zero = jnp.zeros((), BF16)

    @pl.when(e == 0)
    def _():
        acc[...] = jnp.zeros_like(acc)

    n_sub = te // PEER_SUB
    act_bufs = (act_a, act_b)

    def first_matmul(sub):
        act_bufs[sub % 2][...] = _nt(u_ref[sub * PEER_SUB:(sub + 1) * PEER_SUB, :], h_ref[...])

    first_matmul(0)
    for sub in range(n_sub):
        if sub + 1 < n_sub:
            first_matmul(sub + 1)
        act = act_bufs[sub % 2]
        for s in range(slabs_per_sub):
            sl = sub * slabs_per_sub + s
            i1 = e * n_slab + sl
            n1rows = [n1_ref[h, pl.ds(i1, 1), :] for h in range(PEER_HEADS)]
            arows = [a_ref[h, pl.ds(i1, 1), :] for h in range(PEER_HEADS)]
            for lc in range(tp // PEER_LANE_CHUNK):
                lanes = slice(lc * PEER_LANE_CHUNK, (lc + 1) * PEER_LANE_CHUNK)
                tile = (BF16_SUBLANES, PEER_LANE_CHUNK)
                n1b = [jnp.broadcast_to(r[:, lanes], tile).astype(BF16) for r in n1rows]
                ab = [jnp.broadcast_to(r[:, lanes], tile).astype(BF16) for r in arows]
                for g in range(PEER_NKEYS // BF16_SUBLANES):
                    keys = slice(g * BF16_SUBLANES, (g + 1) * BF16_SUBLANES)
                    gate = None
                    for h in range(PEER_HEADS):
                        term = jnp.where(r2_ref[h, keys, lanes] < n1b[h], b_ref[h, keys, lanes], zero) * ab[h]
                        gate = term if gate is None else gate + term
                    rows = slice(s * PEER_NKEYS + g * BF16_SUBLANES, s * PEER_NKEYS + (g + 1) * BF16_SUBLANES)
                    a = act[rows, lanes].astype(BF16)
                    gelu = 0.5 * a * (1.0 + lax.erf(a * math.sqrt(0.5)))
                    rows = slice(sl * PEER_NKEYS + g * BF16_SUBLANES, sl * PEER_NKEYS + (g + 1) * BF16_SUBLANES)
                    ht_scr[rows, lanes] = gate * gelu
    acc[...] += _tn(ht_scr[...], v_ref[...])

    @pl.when(e == pl.num_programs(1) - 1)
    def _():
        x = x1_ref[...] + g2_ref[0] * acc[...]
        o_ref[...] = x * lax.rsqrt(jnp.mean(x * x, axis=-1, keepdims=True) + RMS_EPS) * fg_ref[...]


def _peer(x1, g2, fg, h2, n1, a, r2, bb, u_bf, v_bf, rows_per_mod, tp, te):
    p, d = x1.shape
    n_exp = u_bf.shape[0]
    bpm = rows_per_mod // tp
    tab = lambda: pl.BlockSpec((PEER_HEADS, PEER_NKEYS, tp), lambda i, e: (0, 0, i))
    return pl.pallas_call(
        _peer_kernel,
        grid=(p // tp, n_exp // te),
        in_specs=[pl.BlockSpec((tp, d), lambda i, e: (i, 0), pipeline_mode=pl.Buffered(1)),
                  pl.BlockSpec((1, 1, d), lambda i, e: (i // bpm, 0, 0)),
                  pl.BlockSpec((1, d), lambda i, e: (0, 0)),
                  pl.BlockSpec((tp, d), lambda i, e: (i, 0), pipeline_mode=pl.Buffered(1)),
                  tab(), tab(), tab(), tab(),
                  pl.BlockSpec((te, d), lambda i, e: (e, 0)),
                  pl.BlockSpec((te, d), lambda i, e: (e, 0))],
        out_specs=pl.BlockSpec((tp, d), lambda i, e: (i, 0)),
        out_shape=jax.ShapeDtypeStruct((p, d), F32),
        scratch_shapes=[pltpu.VMEM((tp, d), F32),
                        pltpu.VMEM((te, tp), BF16),
                        pltpu.VMEM((PEER_SUB, tp), F32),
                        pltpu.VMEM((PEER_SUB, tp), F32)],
        compiler_params=_cparams(("parallel", "arbitrary")),
        name="peer_experts",
    )(x1, g2, fg, h2, n1, a, r2, bb, u_bf, v_bf)


class _Tiles(NamedTuple):
    adaln_cols: int
    inproj_rows: int
    inproj_ctx_rows: int
    inproj_cols: int
    attn_queries: int
    outproj_rows: int
    route_tokens: int
    peer_tokens: int
    peer_experts: int


def _tiles(t, ctx_rows):
    return _Tiles(adaln_cols=1024, inproj_rows=min(1024, t), inproj_ctx_rows=min(512, ctx_rows),
                  inproj_cols=2048, attn_queries=min(2048, t), outproj_rows=min(512, t),
                  route_tokens=min(512, t), peer_tokens=min(512, t), peer_experts=1024)


def _rope_tables(t):
    rows = t // GRID_W
    row = jnp.repeat(jnp.arange(rows, dtype=F32), GRID_W)
    colp = jnp.tile(jnp.arange(GRID_W, dtype=F32), rows)
    n_freq = DA_HEAD_DIM // 4
    inv_freq = ROPE_BASE ** (-jnp.arange(n_freq, dtype=F32) / n_freq)
    ar = row[:, None] * inv_freq[None, :]
    ac = colp[:, None] * inv_freq[None, :]
    cos64 = jnp.concatenate([jnp.cos(ar), jnp.cos(ar), jnp.cos(ac), jnp.cos(ac)], axis=-1)
    sin64 = jnp.concatenate([-jnp.sin(ar), jnp.sin(ar), -jnp.sin(ac), jnp.sin(ac)], axis=-1)
    return jnp.tile(cos64, (1, 2)), jnp.tile(sin64, (1, 2))


def _layer(x, c, ctx, c_ctx, w_ada, b_ada, norm1_g, w_in, lb, hg_norm_g, lam, lam_init, da_subln_g,
           w_out, norm2_g, peer_w_query, peer_sub_keys, peer_u, peer_v, final_norm_g):
    b, t, d = x.shape
    tc = ctx.shape[1]
    hg_width = hg_norm_g.shape[0]
    hg_heads = hg_width // HG_EXPAND
    da_heads = da_subln_g.shape[0] // DA_V_DIM
    tiles = _tiles(t, b * tc)

    cond = jnp.zeros((8, d), F32).at[0:b].set(c).at[b].set(c_ctx)
    mod = _adaln(cond, w_ada, b_ada, tiles.adaln_cols)
    sh1, sc1, g1, sh2, sc2, g2 = [m[0:b].reshape(b, 1, d) for m in jnp.split(mod, 6, axis=-1)]
    csh1, csc1 = [m[b:b + 1].reshape(1, 1, d) for m in jnp.split(mod, 6, axis=-1)[0:2]]

    w_in_bf = w_in.astype(BF16)
    n_cols = w_in.shape[1]
    g1n = norm1_g.reshape(1, d)
    parts = _inproj(x.reshape(b * t, d), sc1, sh1, g1n, w_in_bf, t,
                    tiles.inproj_rows, tiles.inproj_cols).reshape(b, t, n_cols)
    cparts = _inproj(ctx.reshape(b * tc, d), csc1, csh1, g1n, w_in_bf, b * tc,
                     tiles.inproj_ctx_rows, tiles.inproj_cols).reshape(b, tc, n_cols)

    hg = _hgrn(cparts, parts, lb, hg_norm_g, hg_heads)
    cos_t, sin_t = _rope_tables(t)
    lam2 = jnp.stack([lam, jnp.asarray(1.0 - lam_init, F32)]).astype(F32)
    da = _diff_attn(cparts, parts, lam2, da_subln_g, cos_t, sin_t, da_heads, 5 * hg_heads, tiles.attn_queries)

    x2 = x.reshape(b * t, d)
    x1, h2 = _outproj(hg.reshape(b * t, -1), da.reshape(b * t, -1), x2, w_out.astype(BF16),
                      g1, sc2, sh2, norm2_g.reshape(1, d), t, tiles.outproj_rows)

    sk = peer_sub_keys.reshape(2 * PEER_HEADS, PEER_NKEYS, -1).astype(BF16)
    n1, a, r2, bb = _route(h2, peer_w_query.astype(BF16), sk, tiles.route_tokens)
    out = _peer(x1, g2, final_norm_g.reshape(1, d), h2, n1, a, r2, bb,
                peer_u.astype(BF16), peer_v.astype(BF16), t, tiles.peer_tokens, tiles.peer_experts)
    return out.reshape(b, t, d)


def kernel(x, c, ctx, c_ctx, w_ada, b_ada, norm1_g, w_in, hg_gamma, hg_norm_g, da_lambda_q1, da_lambda_k1,
           da_lambda_q2, da_lambda_k2, da_subln_g, w_out, norm2_g, peer_w_query, peer_sub_keys, peer_u,
           peer_v, final_norm_g):
    depth = w_ada.shape[0]
    assert depth == 1, "single-layer block: the context stream is never updated"
    lb_all = jnp.cumsum(jax.nn.softmax(hg_gamma.astype(F32), axis=0), axis=0)
    l = 0
    lam_init = 0.8 - 0.6 * math.exp(-0.3 * l)
    lam = (jnp.exp(jnp.sum(da_lambda_q1[l].astype(F32) * da_lambda_k1[l].astype(F32)))
           - jnp.exp(jnp.sum(da_lambda_q2[l].astype(F32) * da_lambda_k2[l].astype(F32))) + lam_init)
    return _layer(x, c, ctx, c_ctx, w_ada[l], b_ada[l], norm1_g[l], w_in[l], lb_all[l], hg_norm_g[l],
                  lam, lam_init, da_subln_g[l], w_out[l], norm2_g[l], peer_w_query[l], peer_sub_keys[l],
                  peer_u[l], peer_v[l], final_norm_g)
```

```python
import math
from typing import NamedTuple

import jax
import jax.numpy as jnp
from jax import lax
from jax.experimental import pallas as pl
from jax.experimental.pallas import tpu as pltpu

F32 = jnp.float32
BF16 = jnp.bfloat16

GRID_W = 64
HG_EXPAND = 128
HG_CHUNK = 64
HG_UNROLL = 4
DA_HEAD_DIM = 64
DA_V_DIM = 128
DA_Q_SUB = 256
ROPE_BASE = 10000.0
PEER_HEADS = 8
PEER_NKEYS = 128
PEER_TOPK = 16
RMS_EPS = 1e-6
LANES = 128
BF16_SUBLANES = 16
EXP_CLAMP = 80.0
V7X_VMEM_BYTES = 64 * 1024 * 1024
VMEM_LIMIT = V7X_VMEM_BYTES - 8 * 1024 * 1024
NEG_INF = float("-inf")


def _nt(a, b):
    return lax.dot_general(a, b, (((1,), (1,)), ((), ())), preferred_element_type=F32)


def _tn(a, b):
    return lax.dot_general(a, b, (((0,), (0,)), ((), ())), preferred_element_type=F32)


def _nn(a, b):
    return jnp.dot(a, b, preferred_element_type=F32)


def _cparams(sem):
    return pltpu.CompilerParams(dimension_semantics=sem, vmem_limit_bytes=VMEM_LIMIT)


def _adaln_kernel(c_ref, w_ref, b_ref, o_ref):
    c = c_ref[...]
    a = (c * jax.nn.sigmoid(c)).astype(BF16)
    o_ref[...] = _nn(a, w_ref[...].astype(BF16)) + b_ref[...]


def _adaln(cond, w, b, tn):
    d, n = w.shape
    return pl.pallas_call(
        _adaln_kernel,
        grid=(n // tn,),
        in_specs=[pl.BlockSpec((8, d), lambda j: (0, 0)),
                  pl.BlockSpec((d, tn), lambda j: (0, j)),
                  pl.BlockSpec((1, tn), lambda j: (0, j))],
        out_specs=pl.BlockSpec((8, tn), lambda j: (0, j)),
        out_shape=jax.ShapeDtypeStruct((8, n), F32),
        compiler_params=_cparams(("arbitrary",)),
        name="adaln",
    )(cond, w, b.reshape(1, n))


def _inproj_kernel(x_ref, sc_ref, sh_ref, g_ref, w_ref, o_ref, h_scr):
    @pl.when(pl.program_id(1) == 0)
    def _():
        x = x_ref[...]
        y = x * lax.rsqrt(jnp.mean(x * x, axis=-1, keepdims=True) + RMS_EPS) * g_ref[...]
        h_scr[...] = (y * (1.0 + sc_ref[0]) + sh_ref[0]).astype(BF16)

    o_ref[...] = _nn(h_scr[...], w_ref[...]).astype(BF16)


def _inproj(x2, sc, sh, g, w, rows_per_mod, tm, tn):
    m, d = x2.shape
    n = w.shape[1]
    bpm = rows_per_mod // tm
    return pl.pallas_call(
        _inproj_kernel,
        grid=(m // tm, n // tn),
        in_specs=[pl.BlockSpec((tm, d), lambda i, j: (i, 0)),
                  pl.BlockSpec((1, 1, d), lambda i, j: (i // bpm, 0, 0)),
                  pl.BlockSpec((1, 1, d), lambda i, j: (i // bpm, 0, 0)),
                  pl.BlockSpec((1, d), lambda i, j: (0, 0)),
                  pl.BlockSpec((d, tn), lambda i, j: (0, j))],
        out_specs=pl.BlockSpec((tm, tn), lambda i, j: (i, j)),
        out_shape=jax.ShapeDtypeStruct((m, n), BF16),
        scratch_shapes=[pltpu.VMEM((tm, d), BF16)],
        compiler_params=_cparams(("parallel", "arbitrary")),
        name="inproj",
    )(x2, sc, sh, g, w)


def _hgrn_gates(z, lb, tri):
    f = lb + (1.0 - lb) * jax.nn.sigmoid(z)
    lf = jnp.log(f)
    hi = lf.astype(BF16)
    lo = (lf - hi.astype(F32)).astype(BF16)
    both = _nn(tri, jnp.concatenate([hi, lo], axis=1))
    w = lf.shape[1]
    return 1.0 - f, both[:, 0:w] + both[:, w:2 * w]


def _hgrn_masks(rev):
    n = 2 * HG_CHUNK
    row = lax.broadcasted_iota(jnp.int32, (n, n), 0)
    col = lax.broadcasted_iota(jnp.int32, (n, n), 1)
    same = (row >= HG_CHUNK) == (col >= HG_CHUNK)
    before = (col >= row) if rev else (row >= col)
    diag = same & before
    cross = (~same) & before
    return jnp.where(diag, 1.0, 0.0).astype(BF16), diag, cross


def _hgrn_block(st, z, v_bf, q, lb, masks, rev):
    c = HG_CHUNK
    tri, diag, cross = masks
    k, cum = _hgrn_gates(z, lb, tri)
    second = lax.broadcasted_iota(jnp.int32, cum.shape, 0) < c if rev else \
        lax.broadcasted_iota(jnp.int32, cum.shape, 0) >= c
    tot_lo = cum[0:1, :] if rev else cum[c - 1:c, :]
    tot_hi = cum[c:c + 1, :] if rev else cum[2 * c - 1:2 * c, :]
    tot_first, tot_second = (tot_hi, tot_lo) if rev else (tot_lo, tot_hi)
    tot_own = jnp.where(lax.broadcasted_iota(jnp.int32, cum.shape, 0) < c, tot_lo, tot_hi)
    k_st = k * jnp.exp(tot_own - cum)
    k_end = k_st * jnp.where(second, 1.0, jnp.exp(tot_second))
    st_new = st * jnp.exp(tot_first + tot_second) + _tn(v_bf, k_end.astype(BF16))
    if q is None:
        return st_new, None, None, None
    mid_lo = cum[c // 2:c // 2 + 1, :] if rev else cum[c // 2 - 1:c // 2, :]
    mid_hi = cum[c + c // 2:c + c // 2 + 1, :] if rev else cum[c + c // 2 - 1:c + c // 2, :]
    rel = cum - jnp.where(lax.broadcasted_iota(jnp.int32, cum.shape, 0) < c, mid_lo, mid_hi)
    qt = q * jnp.exp(jnp.minimum(rel, EXP_CLAMP))
    kt = k * jnp.exp(jnp.minimum(-rel, EXP_CLAMP))
    q_in = q * jnp.exp(cum)
    scores = (jnp.where(diag, _nt(qt.astype(BF16), kt.astype(BF16)), 0.0)
              + jnp.where(cross, _nt(q_in.astype(BF16), k_st.astype(BF16)), 0.0))
    intra = _nn(scores.astype(BF16), v_bf)
    q_start = q_in * jnp.where(second, jnp.exp(tot_first), 1.0)
    inter = _nt(q_start.astype(BF16), st.astype(BF16))
    return st_new, intra, inter, jnp.abs(rel)


def _hgrn_intra_exact(z, v_bf, q, lb, masks, rev, cum_scr, k_scr):
    c = HG_CHUNK
    n = 2 * c
    tri, diag, cross = masks
    k, cum = _hgrn_gates(z, lb, tri)
    rows = lax.broadcasted_iota(jnp.int32, cum.shape, 0)
    if rev:
        cum = cum + jnp.where(rows < c, cum[c:c + 1, :], 0.0)
    else:
        cum = cum + jnp.where(rows >= c, cum[c - 1:c, :], 0.0)
    col = lax.broadcasted_iota(jnp.int32, (n, n), 1)
    cum_scr[...] = cum
    k_scr[...] = k

    def key_column(s, a):
        d = jnp.exp(jnp.minimum(cum - cum_scr[pl.ds(s, 1), :], 0.0))
        return jnp.where(col == s, jnp.sum(q * d * k_scr[pl.ds(s, 1), :], axis=-1, keepdims=True), a)

    a = lax.fori_loop(0, n, key_column, jnp.zeros((n, n), F32))
    return _nn(jnp.where(diag | cross, a, 0.0).astype(BF16), v_bf)


def _silu(x):
    return x * jax.nn.sigmoid(x)


def _hgrn_kernel(lb_ref, ng_ref, czf_ref, czb_ref, ci_ref, q_ref, zf_ref, zb_ref, i_ref, g_ref,
                 o_ref, of_scr, ob_scr, cum_scr, k_scr):
    n = 2 * HG_CHUNK
    n_ctx = czf_ref.shape[0] // n
    n_lat = q_ref.shape[0] // n
    un = math.gcd(HG_UNROLL, n_lat)
    masks_f = _hgrn_masks(False)
    masks_b = _hgrn_masks(True)
    lb_f = lb_ref[0:1, :]
    lb_b = lb_ref[1:2, :]

    def rows_of(block):
        return pl.ds(pl.multiple_of(block * n, n), n)

    def ctx_body(j, carry):
        sf, sb = carry
        rf = rows_of(j)
        rb = rows_of(n_ctx - 1 - j)
        sf = _hgrn_block(sf, czf_ref[rf, :].astype(F32), ci_ref[rf, :], None, lb_f, masks_f, False)[0]
        sb = _hgrn_block(sb, czb_ref[rb, :].astype(F32), ci_ref[rb, :], None, lb_b, masks_b, True)[0]
        return sf, sb

    s0 = jnp.zeros((HG_EXPAND, HG_EXPAND), F32)
    sf, sb = lax.fori_loop(0, n_ctx, ctx_body, (s0, s0))

    def lat_body(j, carry):
        sf, sb = carry
        worst = None
        redo = []
        for u in range(un):
            rf = rows_of(j * un + u)
            rb = rows_of(n_lat - 1 - (j * un + u))
            sf, intra_f, inter_f, ex_f = _hgrn_block(sf, zf_ref[rf, :].astype(F32), i_ref[rf, :],
                                                     _silu(q_ref[rf, :].astype(F32)), lb_f, masks_f, False)
            sb, intra_b, inter_b, ex_b = _hgrn_block(sb, zb_ref[rb, :].astype(F32), i_ref[rb, :],
                                                     _silu(q_ref[rb, :].astype(F32)), lb_b, masks_b, True)
            of_scr[rf, :] = intra_f + inter_f
            ob_scr[rb, :] = intra_b + inter_b
            ex = jnp.maximum(ex_f, ex_b)
            worst = ex if worst is None else jnp.maximum(worst, ex)
            redo.append((rf, rb, inter_f, inter_b))

        @pl.when(jnp.max(worst) > EXP_CLAMP)
        def _():
            for rf, rb, inter_f, inter_b in redo:
                of_scr[rf, :] = inter_f + _hgrn_intra_exact(
                    zf_ref[rf, :].astype(F32), i_ref[rf, :], _silu(q_ref[rf, :].astype(F32)), lb_f, masks_f, False,
                    cum_scr, k_scr)
                ob_scr[rb, :] = inter_b + _hgrn_intra_exact(
                    zb_ref[rb, :].astype(F32), i_ref[rb, :], _silu(q_ref[rb, :].astype(F32)), lb_b, masks_b, True,
                    cum_scr, k_scr)

        return sf, sb

    lax.fori_loop(0, n_lat // un, lat_body, (sf, sb))

    rows = 256
    ng = ng_ref[...]

    def out_body(j, _):
        r = pl.ds(pl.multiple_of(j * rows, rows), rows)
        o = of_scr[r, :] + ob_scr[r, :]
        o = o * lax.rsqrt(jnp.mean(o * o, axis=-1, keepdims=True) + RMS_EPS) * ng
        o_ref[r, :] = (o * _silu(g_ref[r, :].astype(F32))).astype(BF16)
        return 0

    lax.fori_loop(0, q_ref.shape[0] // rows, out_body, 0)


def _hgrn(cparts, parts, lb, norm_g, n_heads):
    b, t, _ = parts.shape
    tc = cparts.shape[1]
    w = HG_EXPAND

    def col(part):
        return lambda bi, hi: (bi, 0, part * n_heads + hi)

    cspec = lambda part: pl.BlockSpec((None, tc, w), col(part))
    lspec = lambda part: pl.BlockSpec((None, t, w), col(part))
    return pl.pallas_call(
        _hgrn_kernel,
        grid=(b, n_heads),
        in_specs=[pl.BlockSpec((2, w), lambda bi, hi: (0, hi)),
                  pl.BlockSpec((1, w), lambda bi, hi: (0, hi)),
                  cspec(1), cspec(2), cspec(3),
                  lspec(0), lspec(1), lspec(2), lspec(3), lspec(4)],
        out_specs=pl.BlockSpec((None, t, w), lambda bi, hi: (bi, 0, hi)),
        out_shape=jax.ShapeDtypeStruct((b, t, n_heads * w), BF16),
        scratch_shapes=[pltpu.VMEM((t, w), F32), pltpu.VMEM((t, w), F32),
                        pltpu.VMEM((2 * HG_CHUNK, w), F32), pltpu.VMEM((2 * HG_CHUNK, w), F32)],
        compiler_params=_cparams(("parallel", "parallel")),
        name="hgrn2",
    )(lb, norm_g.reshape(1, -1), cparts, cparts, cparts, parts, parts, parts, parts, parts)


def _rope(x, cos, sin):
    lane = lax.broadcasted_iota(jnp.int32, x.shape, 1)
    swapped = jnp.where((lane % 32) < 16,
                        pltpu.roll(x, LANES - 16, axis=1),
                        pltpu.roll(x, 16, axis=1))
    return x * cos + swapped * sin


def _da_kernel(lam_ref, q_ref, kc_ref, kl_ref, vc_ref, vl_ref, cq_ref, sq_ref, ck_ref, sk_ref, g_ref,
               o_ref, k_scr, v_scr):
    tc = kc_ref.shape[0]
    t = kl_ref.shape[0]
    dv = DA_V_DIM

    @pl.when(pl.program_id(2) == 0)
    def _():
        lane = lax.broadcasted_iota(jnp.int32, (tc + t, LANES), 1)
        v_scr[:, dv:dv + LANES] = jnp.where(lane == 0, 1.0, 0.0).astype(BF16)
        k_scr[:, 0:tc] = kc_ref[...].astype(F32).T.astype(BF16)
        v_scr[0:tc, 0:dv] = vc_ref[...]
        v_scr[tc:tc + t, 0:dv] = vl_ref[...]
        rows = min(512, t)
        for j in range(t // rows):
            r = slice(j * rows, (j + 1) * rows)
            kr = _rope(kl_ref[r, :].astype(F32), ck_ref[r, :], sk_ref[r, :])
            k_scr[:, tc + j * rows:tc + (j + 1) * rows] = kr.T.astype(BF16)

    lam = lam_ref[0]
    out_scale = lam_ref[1]
    qscale = (DA_HEAD_DIM ** -0.5) * math.log2(math.e)

    def branch(qm):
        s = _nn(qm.astype(BF16), k_scr[...])
        p = jnp.exp2(s - jnp.max(s, axis=-1, keepdims=True)).astype(BF16)
        ov = _nn(p, v_scr[...])
        return ov[:, 0:dv] / ov[:, dv:dv + 1]

    sub = min(DA_Q_SUB, q_ref.shape[0])
    for j in range(q_ref.shape[0] // sub):
        r = slice(j * sub, (j + 1) * sub)
        qr = _rope(q_ref[r, :].astype(F32), cq_ref[r, :], sq_ref[r, :]) * qscale
        lane = lax.broadcasted_iota(jnp.int32, qr.shape, 1)
        o = (branch(jnp.where(lane < DA_HEAD_DIM, qr, 0.0))
             - lam * branch(jnp.where(lane >= DA_HEAD_DIM, qr, 0.0)))
        o = o * lax.rsqrt(jnp.mean(o * o, axis=-1, keepdims=True) + RMS_EPS)
        o_ref[r, :] = (o * g_ref[...] * out_scale).astype(BF16)


def _diff_attn(cparts, parts, lam2, subln_g, cos_t, sin_t, n_heads, col0, tq):
    b, t, _ = parts.shape
    tc = cparts.shape[1]
    w = LANES

    def col(part):
        return lambda bi, hi, qi: (bi, 0, col0 + part * n_heads + hi)

    return pl.pallas_call(
        _da_kernel,
        grid=(b, n_heads, t // tq),
        in_specs=[pl.BlockSpec(memory_space=pltpu.SMEM),
                  pl.BlockSpec((None, tq, w), lambda bi, hi, qi: (bi, qi, col0 + hi)),
                  pl.BlockSpec((None, tc, w), col(1)),
                  pl.BlockSpec((None, t, w), col(1)),
                  pl.BlockSpec((None, tc, w), col(2)),
                  pl.BlockSpec((None, t, w), col(2)),
                  pl.BlockSpec((tq, w), lambda bi, hi, qi: (qi, 0)),
                  pl.BlockSpec((tq, w), lambda bi, hi, qi: (qi, 0)),
                  pl.BlockSpec((t, w), lambda bi, hi, qi: (0, 0)),
                  pl.BlockSpec((t, w), lambda bi, hi, qi: (0, 0)),
                  pl.BlockSpec((1, w), lambda bi, hi, qi: (0, hi))],
        out_specs=pl.BlockSpec((None, tq, w), lambda bi, hi, qi: (bi, qi, hi)),
        out_shape=jax.ShapeDtypeStruct((b, t, n_heads * w), BF16),
        scratch_shapes=[pltpu.VMEM((w, tc + t), BF16), pltpu.VMEM((tc + t, DA_V_DIM + w), BF16)],
        compiler_params=_cparams(("parallel", "parallel", "arbitrary")),
        name="diff_attn",
    )(lam2, parts, cparts, parts, cparts, parts, cos_t, sin_t, cos_t, sin_t, subln_g.reshape(1, -1))


def _outproj_kernel(hg_ref, da_ref, x_ref, wa_ref, wb_ref, g1_ref, sc_ref, sh_ref, ng_ref, x1_ref, h2_ref):
    mix = _nn(hg_ref[...], wa_ref[...]) + _nn(da_ref[...], wb_ref[...])
    x1 = x_ref[...] + g1_ref[0] * mix
    x1_ref[...] = x1
    y = x1 * lax.rsqrt(jnp.mean(x1 * x1, axis=-1, keepdims=True) + RMS_EPS) * ng_ref[...]
    h2_ref[...] = (y * (1.0 + sc_ref[0]) + sh_ref[0]).astype(BF16)


def _outproj(hg, da, x2, w_out, g1, sc2, sh2, norm_g, rows_per_mod, tm):
    m, d = x2.shape
    half = hg.shape[1]
    bpm = rows_per_mod // tm
    mod = pl.BlockSpec((1, 1, d), lambda i: (i // bpm, 0, 0))
    return pl.pallas_call(
        _outproj_kernel,
        grid=(m // tm,),
        in_specs=[pl.BlockSpec((tm, half), lambda i: (i, 0)),
                  pl.BlockSpec((tm, half), lambda i: (i, 0)),
                  pl.BlockSpec((tm, d), lambda i: (i, 0)),
                  pl.BlockSpec((half, d), lambda i: (0, 0)),
                  pl.BlockSpec((half, d), lambda i: (1, 0)),
                  mod, mod, mod,
                  pl.BlockSpec((1, d), lambda i: (0, 0))],
        out_specs=[pl.BlockSpec((tm, d), lambda i: (i, 0)),
                   pl.BlockSpec((tm, d), lambda i: (i, 0))],
        out_shape=[jax.ShapeDtypeStruct((m, d), F32), jax.ShapeDtypeStruct((m, d), BF16)],
        compiler_params=_cparams(("parallel",)),
        name="outproj",
    )(hg, da, x2, w_out, w_out, g1, sc2, sh2, norm_g)


def _top_values(s, n):
    vals = []
    rank = jnp.full(s.shape, float(n), F32)
    for r in range(n):
        m = jnp.max(s, axis=0, keepdims=True)
        hit = s == m
        vals.append(m)
        rank = jnp.where(hit, float(r), rank)
        s = jnp.where(hit, NEG_INF, s)
    return vals, rank


def _route_tables(s1, s2):
    kk = PEER_TOPK
    v1, rank1 = _top_values(s1, kk)
    v2, rank2 = _top_values(s2, kk)
    sv2 = jnp.concatenate(v2, axis=0)
    cands = [v1[0] + sv2]
    cands += [v1[r] + sv2[0:8, :] for r in range(1, 8)]
    cands.append(jnp.concatenate(v1[8:16], axis=0) + v2[0])
    top = v1[0] + v2[0]
    z = jnp.zeros_like(top)
    tau = top
    work = cands
    for _ in range(kk):
        m8 = work[1]
        for cnd in work[2:]:
            m8 = jnp.maximum(m8, cnd)
        tau = jnp.maximum(jnp.max(work[0], axis=0, keepdims=True), jnp.max(m8, axis=0, keepdims=True))
        z = z + jnp.exp(tau - top)
        work = [jnp.where(cnd == tau, NEG_INF, cnd) for cnd in work]
    cnt = [jnp.sum(jnp.where(cands[r] >= tau, 1.0, 0.0), axis=0, keepdims=True) for r in range(8)]
    tail = jnp.where(cands[8] >= tau, 1.0, 0.0)
    cnt += [tail[r:r + 1, :] for r in range(8)]
    n1 = jnp.zeros_like(s1)
    for r in range(kk):
        n1 = jnp.where(rank1 == float(r), cnt[r], n1)
    return n1, jnp.exp(s1 - v1[0]), rank2.astype(BF16), (jnp.exp(s2 - v2[0]) / z).astype(BF16)


def _route_kernel(h_ref, wq_ref, sk_ref, n1_ref, a_ref, r2_ref, b_ref):
    tp = h_ref.shape[0]
    q = _nn(h_ref[...], wq_ref[...]).astype(BF16)
    for h in range(PEER_HEADS):
        s1 = _nt(sk_ref[2 * h], q[:, (2 * h) * LANES:(2 * h + 1) * LANES])
        s2 = _nt(sk_ref[2 * h + 1], q[:, (2 * h + 1) * LANES:(2 * h + 2) * LANES])
        for c in range(tp // LANES):
            lanes = slice(c * LANES, (c + 1) * LANES)
            n1, a, r2, b = _route_tables(s1[:, lanes], s2[:, lanes])
            n1_ref[h, :, lanes] = n1
            a_ref[h, :, lanes] = a
            r2_ref[h, :, lanes] = r2
            b_ref[h, :, lanes] = b


def _route(h2, wq, sk, tp):
    p, d = h2.shape
    nq = wq.shape[1]
    tab = lambda: pl.BlockSpec((PEER_HEADS, PEER_NKEYS, tp), lambda i: (0, 0, i))
    return pl.pallas_call(
        _route_kernel,
        grid=(p // tp,),
        in_specs=[pl.BlockSpec((tp, d), lambda i: (i, 0)),
                  pl.BlockSpec((d, nq), lambda i: (0, 0)),
                  pl.BlockSpec((2 * PEER_HEADS, PEER_NKEYS, LANES), lambda i: (0, 0, 0))],
        out_specs=[tab(), tab(), tab(), tab()],
        out_shape=[jax.ShapeDtypeStruct((PEER_HEADS, PEER_NKEYS, p), F32),
                   jax.ShapeDtypeStruct((PEER_HEADS, PEER_NKEYS, p), F32),
                   jax.ShapeDtypeStruct((PEER_HEADS, PEER_NKEYS, p), BF16),
                   jax.ShapeDtypeStruct((PEER_HEADS, PEER_NKEYS, p), BF16)],
        compiler_params=_cparams(("parallel",)),
        name="peer_route",
    )(h2, wq, sk)


PEER_SUB = 512
PEER_LANE_CHUNK = 256


def _peer_kernel(x1_ref, g2_ref, fg_ref, h_ref, n1_ref, a_ref, r2_ref, b_ref, u_ref, v_ref, o_ref,
                 acc, ht_scr, act_a, act_b):
    e = pl.program_id(1)
    te, tp = ht_scr.shape
    n_slab = te // PEER_NKEYS
    slabs_per_sub = PEER_SUB // PEER_NKEYS
    zero = jnp.zeros((), BF16)

    @pl.when(e == 0)
    def _():
        acc[...] = jnp.zeros_like(acc)

    n_sub = te // PEER_SUB
    act_bufs = (act_a, act_b)

    def first_matmul(sub):
        act_bufs[sub % 2][...] = _nt(u_ref[sub * PEER_SUB:(sub + 1) * PEER_SUB, :], h_ref[...])

    first_matmul(0)
    for sub in range(n_sub):
        if sub + 1 < n_sub:
            first_matmul(sub + 1)
        act = act_bufs[sub % 2]
        for s in range(slabs_per_sub):
            sl = sub * slabs_per_sub + s
            i1 = e * n_slab + sl
            n1rows = [n1_ref[h, pl.ds(i1, 1), :] for h in range(PEER_HEADS)]
            arows = [a_ref[h, pl.ds(i1, 1), :] for h in range(PEER_HEADS)]
            for lc in range(tp // PEER_LANE_CHUNK):
                lanes = slice(lc * PEER_LANE_CHUNK, (lc + 1) * PEER_LANE_CHUNK)
                tile = (BF16_SUBLANES, PEER_LANE_CHUNK)
                n1b = [jnp.broadcast_to(r[:, lanes], tile).astype(BF16) for r in n1rows]
                ab = [jnp.broadcast_to(r[:, lanes], tile).astype(BF16) for r in arows]
                for g in range(PEER_NKEYS // BF16_SUBLANES):
                    keys = slice(g * BF16_SUBLANES, (g + 1) * BF16_SUBLANES)
                    gate = None
                    for h in range(PEER_HEADS):
                        term = jnp.where(r2_ref[h, keys, lanes] < n1b[h], b_ref[h, keys, lanes], zero) * ab[h]
                        gate = term if gate is None else gate + term
                    rows = slice(s * PEER_NKEYS + g * BF16_SUBLANES, s * PEER_NKEYS + (g + 1) * BF16_SUBLANES)
                    a = act[rows, lanes].astype(BF16)
                    gelu = 0.5 * a * (1.0 + lax.erf(a * math.sqrt(0.5)))
                    rows = slice(sl * PEER_NKEYS + g * BF16_SUBLANES, sl * PEER_NKEYS + (g + 1) * BF16_SUBLANES)
                    ht_scr[rows, lanes] = gate * gelu
    acc[...] += _tn(ht_scr[...], v_ref[...])

    @pl.when(e == pl.num_programs(1) - 1)
    def _():
        x = x1_ref[...] + g2_ref[0] * acc[...]
        o_ref[...] = x * lax.rsqrt(jnp.mean(x * x, axis=-1, keepdims=True) + RMS_EPS) * fg_ref[...]


def _peer(x1, g2, fg, h2, n1, a, r2, bb, u_bf, v_bf, rows_per_mod, tp, te):
    p, d = x1.shape
    n_exp = u_bf.shape[0]
    bpm = rows_per_mod // tp
    tab = lambda: pl.BlockSpec((PEER_HEADS, PEER_NKEYS, tp), lambda i, e: (0, 0, i))
    return pl.pallas_call(
        _peer_kernel,
        grid=(p // tp, n_exp // te),
        in_specs=[pl.BlockSpec((tp, d), lambda i, e: (i, 0), pipeline_mode=pl.Buffered(1)),
                  pl.BlockSpec((1, 1, d), lambda i, e: (i // bpm, 0, 0)),
                  pl.BlockSpec((1, d), lambda i, e: (0, 0)),
                  pl.BlockSpec((tp, d), lambda i, e: (i, 0), pipeline_mode=pl.Buffered(1)),
                  tab(), tab(), tab(), tab(),
                  pl.BlockSpec((te, d), lambda i, e: (e, 0)),
                  pl.BlockSpec((te, d), lambda i, e: (e, 0))],
        out_specs=pl.BlockSpec((tp, d), lambda i, e: (i, 0)),
        out_shape=jax.ShapeDtypeStruct((p, d), F32),
        scratch_shapes=[pltpu.VMEM((tp, d), F32),
                        pltpu.VMEM((te, tp), BF16),
                        pltpu.VMEM((PEER_SUB, tp), F32),
                        pltpu.VMEM((PEER_SUB, tp), F32)],
        compiler_params=_cparams(("parallel", "arbitrary")),
        name="peer_experts",
    )(x1, g2, fg, h2, n1, a, r2, bb, u_bf, v_bf)


class _Tiles(NamedTuple):
    adaln_cols: int
    inproj_rows: int
    inproj_ctx_rows: int
    inproj_cols: int
    attn_queries: int
    outproj_rows: int
    route_tokens: int
    peer_tokens: int
    peer_experts: int


def _tiles(t, ctx_rows):
    return _Tiles(adaln_cols=1024, inproj_rows=min(1024, t), inproj_ctx_rows=min(512, ctx_rows),
                  inproj_cols=2048, attn_queries=min(2048, t), outproj_rows=min(512, t),
                  route_tokens=min(512, t), peer_tokens=min(512, t), peer_experts=1024)


def _rope_tables(t):
    rows = t // GRID_W
    row = jnp.repeat(jnp.arange(rows, dtype=F32), GRID_W)
    colp = jnp.tile(jnp.arange(GRID_W, dtype=F32), rows)
    n_freq = DA_HEAD_DIM // 4
    inv_freq = ROPE_BASE ** (-jnp.arange(n_freq, dtype=F32) / n_freq)
    ar = row[:, None] * inv_freq[None, :]
    ac = colp[:, None] * inv_freq[None, :]
    cos64 = jnp.concatenate([jnp.cos(ar), jnp.cos(ar), jnp.cos(ac), jnp.cos(ac)], axis=-1)
    sin64 = jnp.concatenate([-jnp.sin(ar), jnp.sin(ar), -jnp.sin(ac), jnp.sin(ac)], axis=-1)
    return jnp.tile(cos64, (1, 2)), jnp.tile(sin64, (1, 2))


def _layer(x, c, ctx, c_ctx, w_ada, b_ada, norm1_g, w_in, lb, hg_norm_g, lam, lam_init, da_subln_g,
           w_out, norm2_g, peer_w_query, peer_sub_keys, peer_u, peer_v, final_norm_g):
    b, t, d = x.shape
    tc = ctx.shape[1]
    hg_width = hg_norm_g.shape[0]
    hg_heads = hg_width // HG_EXPAND
    da_heads = da_subln_g.shape[0] // DA_V_DIM
    tiles = _tiles(t, b * tc)

    cond = jnp.zeros((8, d), F32).at[0:b].set(c).at[b].set(c_ctx)
    mod = _adaln(cond, w_ada, b_ada, tiles.adaln_cols)
    sh1, sc1, g1, sh2, sc2, g2 = [m[0:b].reshape(b, 1, d) for m in jnp.split(mod, 6, axis=-1)]
    csh1, csc1 = [m[b:b + 1].reshape(1, 1, d) for m in jnp.split(mod, 6, axis=-1)[0:2]]

    w_in_bf = w_in.astype(BF16)
    n_cols = w_in.shape[1]
    g1n = norm1_g.reshape(1, d)
    parts = _inproj(x.reshape(b * t, d), sc1, sh1, g1n, w_in_bf, t,
                    tiles.inproj_rows, tiles.inproj_cols).reshape(b, t, n_cols)
    cparts = _inproj(ctx.reshape(b * tc, d), csc1, csh1, g1n, w_in_bf, b * tc,
                     tiles.inproj_ctx_rows, tiles.inproj_cols).reshape(b, tc, n_cols)

    hg = _hgrn(cparts, parts, lb, hg_norm_g, hg_heads)
    cos_t, sin_t = _rope_tables(t)
    lam2 = jnp.stack([lam, jnp.asarray(1.0 - lam_init, F32)]).astype(F32)
    da = _diff_attn(cparts, parts, lam2, da_subln_g, cos_t, sin_t, da_heads, 5 * hg_heads, tiles.attn_queries)

    x2 = x.reshape(b * t, d)
    x1, h2 = _outproj(hg.reshape(b * t, -1), da.reshape(b * t, -1), x2, w_out.astype(BF16),
                      g1, sc2, sh2, norm2_g.reshape(1, d), t, tiles.outproj_rows)

    sk = peer_sub_keys.reshape(2 * PEER_HEADS, PEER_NKEYS, -1).astype(BF16)
    n1, a, r2, bb = _route(h2, peer_w_query.astype(BF16), sk, tiles.route_tokens)
    out = _peer(x1, g2, final_norm_g.reshape(1, d), h2, n1, a, r2, bb,
                peer_u.astype(BF16), peer_v.astype(BF16), t, tiles.peer_tokens, tiles.peer_experts)
    return out.reshape(b, t, d)


def kernel(x, c, ctx, c_ctx, w_ada, b_ada, norm1_g, w_in, hg_gamma, hg_norm_g, da_lambda_q1, da_lambda_k1,
           da_lambda_q2, da_lambda_k2, da_subln_g, w_out, norm2_g, peer_w_query, peer_sub_keys, peer_u,
           peer_v, final_norm_g):
    depth = w_ada.shape[0]
    assert depth == 1, "single-layer block: the context stream is never updated"
    lb_all = jnp.cumsum(jax.nn.softmax(hg_gamma.astype(F32), axis=0), axis=0)
    l = 0
    lam_init = 0.8 - 0.6 * math.exp(-0.3 * l)
    lam = (jnp.exp(jnp.sum(da_lambda_q1[l].astype(F32) * da_lambda_k1[l].astype(F32)))
           - jnp.exp(jnp.sum(da_lambda_q2[l].astype(F32) * da_lambda_k2[l].astype(F32))) + lam_init)
    return _layer(x, c, ctx, c_ctx, w_ada[l], b_ada[l], norm1_g[l], w_in[l], lb_all[l], hg_norm_g[l],
                  lam, lam_init, da_subln_g[l], w_out[l], norm2_g[l], peer_w_query[l], peer_sub_keys[l],
                  peer_u[l], peer_v[l], final_norm_g)
```
